```python
import jax
import jax.numpy as jnp
from jax import lax
import numpy as np

D_MODEL = 4096
BATCH = 1
SEQ = 16384
DEPTH = 4

CTX_LEN = 256
GRID_W = 64
HEAD_DIM = 128
N_MIX_HEADS = D_MODEL // HEAD_DIM
A_HEADS = N_MIX_HEADS // 2
A_KV_HEADS = A_HEADS // 4
B_HEADS = N_MIX_HEADS // 2
B_KV_HEADS = B_HEADS // 4
C_HEADS = N_MIX_HEADS // 2
D_CH = D_MODEL // 2
ROPE_PAIRS = HEAD_DIM // 4
ROPE_THETA = 10000.0
WINDOW = 128
Q_BLOCK = 128
NA_KH = 8
NA_KW = 16
CONV_W = 31
ADA_RANK = 256
N_MOD = 6
N_GROUPS = 4
EXP_PER_GROUP = 8
N_EXPERTS = N_GROUPS * EXP_PER_GROUP
EXPERT_TOPK = 2
D_EXPERT = (D_MODEL * 3) // 32
TOK_BLOCK = 128
EPS = 1e-6
N_EVEN = (DEPTH + 1) // 2
N_ODD = DEPTH // 2
AB_WIDTHS = (A_HEADS * HEAD_DIM, A_KV_HEADS * HEAD_DIM, A_KV_HEADS * HEAD_DIM,
             B_HEADS * HEAD_DIM, B_KV_HEADS * HEAD_DIM, B_KV_HEADS * HEAD_DIM)
CD_WIDTHS = (C_HEADS * HEAD_DIM, C_HEADS * HEAD_DIM, C_HEADS * HEAD_DIM, 2 * D_CH)
AB_IN = sum(AB_WIDTHS)
CD_IN = sum(CD_WIDTHS)
AB_OUT = (A_HEADS + B_HEADS) * HEAD_DIM
CD_OUT = C_HEADS * HEAD_DIM + D_CH

kernel_name = 'hybrid_flow_backbone'


def _rmsnorm(x, g):
    xf = x.astype(jnp.float32)
    r = lax.rsqrt(jnp.mean(xf * xf, axis=-1, keepdims=True) + EPS)
    return (xf * r).astype(x.dtype) * g


def _layernorm(x, g, b):
    xf = x.astype(jnp.float32)
    mu = jnp.mean(xf, axis=-1, keepdims=True)
    var = jnp.mean(jnp.square(xf - mu), axis=-1, keepdims=True)
    return ((xf - mu) * lax.rsqrt(var + EPS)).astype(x.dtype) * g + b


def _split_cols(p, widths):
    idx = np.cumsum(widths)[:-1].tolist()
    return jnp.split(p, idx, axis=-1)


def _heads(t):
    return t.reshape(t.shape[:-1] + (-1, HEAD_DIM))


def _rope_tables(L):
    t = jnp.arange(L)
    row = (t // GRID_W).astype(jnp.float32)
    col = (t % GRID_W).astype(jnp.float32)
    inv = ROPE_THETA ** (-jnp.arange(ROPE_PAIRS, dtype=jnp.float32) / ROPE_PAIRS)
    ang = jnp.stack([row[:, None] * inv, col[:, None] * inv], axis=1)
    return jnp.cos(ang), jnp.sin(ang)


def _rope(x, cos, sin):
    xs = x.reshape(x.shape[:-1] + (2, 2, ROPE_PAIRS)).astype(jnp.float32)
    c = cos[None, :, None]
    s = sin[None, :, None]
    a = xs[..., 0, :]
    b = xs[..., 1, :]
    out = jnp.stack([a * c - b * s, b * c + a * s], axis=-2)
    return out.reshape(x.shape).astype(x.dtype)


def _ctx_attn(q, k, v, sink):
    B, Lc, H, hd = q.shape
    Hkv = k.shape[2]
    G = H // Hkv
    qg = q.reshape(B, Lc, Hkv, G, hd)
    s = jnp.einsum('bqkgd,bskd->bkgqs', qg, k).astype(jnp.float32)
    if sink is not None:
        sc = sink.reshape(Hkv, G)[None, :, :, None, None].astype(jnp.float32)
        s = jnp.concatenate([s, jnp.broadcast_to(sc, s.shape[:-1] + (1,))], axis=-1)
    p = jax.nn.softmax(s, axis=-1)[..., :Lc].astype(v.dtype)
    o = jnp.einsum('bkgqs,bskd->bqkgd', p, v)
    return o.reshape(B, Lc, H * hd)


def _global_gqa(q, k_all, v_all):
    B, L, H, hd = q.shape
    Hkv = k_all.shape[2]
    G = H // Hkv
    nb = L // Q_BLOCK
    qb = q.reshape(B, nb, Q_BLOCK, Hkv, G, hd).transpose(1, 0, 2, 3, 4, 5)

    def blk(qq):
        s = jnp.einsum('bqkgd,bskd->bkgqs', qq, k_all).astype(jnp.float32)
        p = jax.nn.softmax(s, axis=-1).astype(v_all.dtype)
        return jnp.einsum('bkgqs,bskd->bqkgd', p, v_all)

    o = lax.map(blk, qb)
    return o.transpose(1, 0, 2, 3, 4, 5).reshape(B, L, H * hd)


def _window_gqa(q, k, v, kc, vc, sink):
    B, L, H, hd = q.shape
    Hkv = k.shape[2]
    G = H // Hkv
    Lc = kc.shape[1]
    nb = L // Q_BLOCK
    span = Q_BLOCK + 2 * WINDOW
    kp = jnp.pad(k, ((0, 0), (WINDOW, WINDOW), (0, 0), (0, 0)))
    vp = jnp.pad(v, ((0, 0), (WINDOW, WINDOW), (0, 0), (0, 0)))
    qb = q.reshape(B, nb, Q_BLOCK, Hkv, G, hd).transpose(1, 0, 2, 3, 4, 5)
    sink_col = sink.reshape(Hkv, G)[None, :, :, None, None].astype(jnp.float32)
    rel = jnp.arange(span)[None, :] - WINDOW - jnp.arange(Q_BLOCK)[:, None]
    in_band = jnp.abs(rel) <= WINDOW

    def blk(args):
        b, qq = args
        start = b * Q_BLOCK
        ks = lax.dynamic_slice_in_dim(kp, start, span, axis=1)
        vs = lax.dynamic_slice_in_dim(vp, start, span, axis=1)
        j = start - WINDOW + jnp.arange(span)
        mask = in_band & ((j >= 0) & (j < L))[None, :]
        s_loc = jnp.einsum('bqkgd,bskd->bkgqs', qq, ks).astype(jnp.float32)
        s_loc = jnp.where(mask, s_loc, -jnp.inf)
        s_ctx = jnp.einsum('bqkgd,bskd->bkgqs', qq, kc).astype(jnp.float32)
        s_snk = jnp.broadcast_to(sink_col, s_ctx.shape[:-1] + (1,))
        p = jax.nn.softmax(jnp.concatenate([s_loc, s_ctx, s_snk], axis=-1), axis=-1)
        p_loc = p[..., :span].astype(v.dtype)
        p_ctx = p[..., span:span + Lc].astype(v.dtype)
        return (jnp.einsum('bkgqs,bskd->bqkgd', p_loc, vs)
                + jnp.einsum('bkgqs,bskd->bqkgd', p_ctx, vc))

    o = lax.map(blk, (jnp.arange(nb), qb))
    return o.transpose(1, 0, 2, 3, 4, 5).reshape(B, L, H * hd)


def _neighbourhood_attn(q, k, v, kc, vc, rpb):
    B, L, H, hd = q.shape
    rows = L // GRID_W
    kh = min(NA_KH, rows)
    kw = NA_KW
    Lc = kc.shape[1]
    qg = q.reshape(B, rows, GRID_W, H, hd).transpose(1, 0, 2, 3, 4)
    kg = k.reshape(B, rows, GRID_W, H, hd)
    vg = v.reshape(B, rows, GRID_W, H, hd)
    col = jnp.arange(GRID_W)
    cs = jnp.clip(col - kw // 2, 0, GRID_W - kw)
    col_idx = cs[:, None] + jnp.arange(kw)[None, :]
    col_bias_idx = col_idx - col[:, None] + NA_KW - 1
    rpb_cols = jnp.take(rpb, col_bias_idx, axis=2)
    n_loc = kh * kw

    def blk(args):
        r, qq = args
        rs = jnp.clip(r - kh // 2, 0, rows - kh)
        krow = lax.dynamic_slice_in_dim(kg, rs, kh, axis=1)
        vrow = lax.dynamic_slice_in_dim(vg, rs, kh, axis=1)
        kq = jnp.take(krow, col_idx, axis=2)
        vq = jnp.take(vrow, col_idx, axis=2)
        row_idx = rs + jnp.arange(kh) - r + NA_KH - 1
        bias = jnp.take(rpb_cols, row_idx, axis=1).transpose(0, 2, 1, 3)
        s_loc = jnp.einsum('bqhd,brqwhd->bhqrw', qq, kq).astype(jnp.float32) + bias[None].astype(jnp.float32)
        s_loc = s_loc.reshape(B, H, GRID_W, n_loc)
        s_ctx = jnp.einsum('bqhd,bkhd->bhqk', qq, kc).astype(jnp.float32)
        p = jax.nn.softmax(jnp.concatenate([s_loc, s_ctx], axis=-1), axis=-1)
        p_loc = p[..., :n_loc].reshape(B, H, GRID_W, kh, kw).astype(v.dtype)
        p_ctx = p[..., n_loc:n_loc + Lc].astype(v.dtype)
        return (jnp.einsum('bhqrw,brqwhd->bqhd', p_loc, vq)
                + jnp.einsum('bhqk,bkhd->bqhd', p_ctx, vc))

    o = lax.map(blk, (jnp.arange(rows), qg))
    return o.transpose(1, 0, 2, 3, 4).reshape(B, L, H * hd)


def _conformer_conv(g, dw_w, dw_b, ln_g, ln_b):
    a, gate = jnp.split(g, 2, axis=-1)
    u = a * jax.nn.sigmoid(gate)
    y = lax.conv_general_dilated(u, dw_w[:, None, :], window_strides=(1,),
                                 padding=((CONV_W // 2, CONV_W // 2),),
                                 dimension_numbers=('NWC', 'WIO', 'NWC'),
                                 feature_group_count=D_CH) + dw_b
    return jax.nn.silu(_layernorm(y, ln_g, ln_b))


def _mixer_ab(hc, hx, w_in, w_out, q_g, k_g, sink, cos, sin, need_ctx):
    scale = HEAD_DIM ** -0.5
    qa, ka, va, qb, kb, vb = [_heads(t) for t in _split_cols(hx @ w_in, AB_WIDTHS)]
    qa_c, ka_c, va_c, qb_c, kb_c, vb_c = [_heads(t) for t in _split_cols(hc @ w_in, AB_WIDTHS)]
    ka_c = _rmsnorm(ka_c, k_g)
    qa = _rope(_rmsnorm(qa, q_g), cos, sin) * scale
    ka = _rope(_rmsnorm(ka, k_g), cos, sin)
    oa = _global_gqa(qa, jnp.concatenate([ka_c, ka], axis=1), jnp.concatenate([va_c, va], axis=1))
    ob = _window_gqa(_rope(qb, cos, sin) * scale, _rope(kb, cos, sin), vb, kb_c, vb_c, sink)
    yx = jnp.concatenate([oa, ob], axis=-1) @ w_out
    yc = None
    if need_ctx:
        oa_c = _ctx_attn(_rmsnorm(qa_c, q_g) * scale, ka_c, va_c, None)
        ob_c = _ctx_attn(qb_c * scale, kb_c, vb_c, sink)
        yc = jnp.concatenate([oa_c, ob_c], axis=-1) @ w_out
    return yc, yx


def _mixer_cd(hc, hx, w_in, w_out, rpb, dw_w, dw_b, ln_g, ln_b, need_ctx):
    scale = HEAD_DIM ** -0.5
    qx, kx, vx, gx = _split_cols(hx @ w_in, CD_WIDTHS)
    qc, kc, vc, gc = _split_cols(hc @ w_in, CD_WIDTHS)
    kc = _heads(kc)
    vc = _heads(vc)
    oc = _neighbourhood_attn(_heads(qx) * scale, _heads(kx), _heads(vx), kc, vc, rpb)
    od = _conformer_conv(gx, dw_w, dw_b, ln_g, ln_b)
    yx = jnp.concatenate([oc, od], axis=-1) @ w_out
    yc = None
    if need_ctx:
        oc_c = _ctx_attn(_heads(qc) * scale, kc, vc, None)
        od_c = _conformer_conv(gc, dw_w, dw_b, ln_g, ln_b)
        yc = jnp.concatenate([oc_c, od_c], axis=-1) @ w_out
    return yc, yx


def _moe(h, w_rg, w_re, w_g, w_u, w_d):
    B, L, D = h.shape
    tb = h.reshape(-1, TOK_BLOCK, D)

    def blk(t):
        glog = (t @ w_rg).astype(jnp.float32)
        gprob = jax.nn.softmax(glog, axis=-1)
        gsel = jnp.argmax(glog, axis=-1)
        elog_all = jnp.einsum('td,gde->tge', t, w_re).astype(jnp.float32)
        elog = jnp.take_along_axis(elog_all, gsel[:, None, None], axis=1)[:, 0]
        ev, ei = lax.top_k(elog, EXPERT_TOPK)
        wts = jax.nn.softmax(ev, axis=-1) * jnp.take_along_axis(gprob, gsel[:, None], axis=1)
        eid = gsel[:, None] * EXP_PER_GROUP + ei
        comb = jnp.sum(jax.nn.one_hot(eid, N_EXPERTS, dtype=jnp.float32) * wts[..., None], axis=1)
        hg = jnp.einsum('td,edf->tef', t, w_g)
        hu = jnp.einsum('td,edf->tef', t, w_u)
        act = jax.nn.silu(hg) * hu * comb[:, :, None].astype(t.dtype)
        return jnp.einsum('tef,efd->td', act, w_d)

    return lax.map(blk, tb).reshape(B, L, D)


def _modulation(cond, down, up, bias):
    m = (jax.nn.silu(cond) @ down) @ up + bias
    return m.reshape(cond.shape[0], N_MOD, D_MODEL)


def _modulate(h, shift, scale):
    return h * (1 + scale[:, None, :]) + shift[:, None, :]


def setup_inputs(seed: int = 0) -> dict:
    key = jax.random.key(seed)
    ks = jax.random.split(key, 32)
    f32 = jnp.float32
    D = D_MODEL

    def nrm(k, shape, s):
        return jax.random.normal(k, shape, f32) * s

    return {
        'x': nrm(ks[0], (BATCH, SEQ, D), 1.0),
        'c': nrm(ks[1], (BATCH, D), 1.0),
        'ctx': nrm(ks[2], (BATCH, CTX_LEN, D), 1.0),
        'c_ctx': nrm(ks[3], (D,), 1.0),
        'ada_down': nrm(ks[4], (DEPTH, D, ADA_RANK), D ** -0.5),
        'ada_up': nrm(ks[5], (DEPTH, ADA_RANK, N_MOD * D), 0.3 * ADA_RANK ** -0.5),
        'ada_bias': nrm(ks[6], (DEPTH, N_MOD * D), 0.01),
        'norm1_g': 1.0 + nrm(ks[7], (DEPTH, D), 0.01),
        'norm2_g': 1.0 + nrm(ks[8], (DEPTH, D), 0.01),
        'w_in_ab': nrm(ks[9], (N_EVEN, D, AB_IN), D ** -0.5),
        'w_out_ab': nrm(ks[10], (N_EVEN, AB_OUT, D), AB_OUT ** -0.5),
        'qnorm_a': 1.0 + nrm(ks[11], (N_EVEN, HEAD_DIM), 0.01),
        'knorm_a': 1.0 + nrm(ks[12], (N_EVEN, HEAD_DIM), 0.01),
        'sink_b': nrm(ks[13], (N_EVEN, B_HEADS), 0.1),
        'w_in_cd': nrm(ks[14], (N_ODD, D, CD_IN), D ** -0.5),
        'w_out_cd': nrm(ks[15], (N_ODD, CD_OUT, D), CD_OUT ** -0.5),
        'rpb_c': nrm(ks[16], (N_ODD, C_HEADS, 2 * NA_KH - 1, 2 * NA_KW - 1), 0.02),
        'dw_w': nrm(ks[17], (N_ODD, CONV_W, D_CH), CONV_W ** -0.5),
        'dw_b': nrm(ks[18], (N_ODD, D_CH), 0.01),
        'ln_d_g': 1.0 + nrm(ks[19], (N_ODD, D_CH), 0.01),
        'ln_d_b': nrm(ks[20], (N_ODD, D_CH), 0.01),
        'router_group': nrm(ks[21], (DEPTH, D, N_GROUPS), D ** -0.5),
        'router_expert': nrm(ks[22], (DEPTH, N_GROUPS, D, EXP_PER_GROUP), D ** -0.5),
        'w_gate': nrm(ks[23], (DEPTH, N_EXPERTS, D, D_EXPERT), D ** -0.5),
        'w_up': nrm(ks[24], (DEPTH, N_EXPERTS, D, D_EXPERT), D ** -0.5),
        'w_down': nrm(ks[25], (DEPTH, N_EXPERTS, D_EXPERT, D), D_EXPERT ** -0.5),
        'final_g': 1.0 + nrm(ks[26], (D,), 0.01),
    }


def reference(x, c, ctx, c_ctx, ada_down, ada_up, ada_bias, norm1_g, norm2_g,
              w_in_ab, w_out_ab, qnorm_a, knorm_a, sink_b,
              w_in_cd, w_out_cd, rpb_c, dw_w, dw_b, ln_d_g, ln_d_b,
              router_group, router_expert, w_gate, w_up, w_down, final_g):
    L = x.shape[1]
    cos, sin = _rope_tables(L)
    zx = x
    zc = ctx
    for layer in range(DEPTH):
        need_ctx = layer < DEPTH - 1
        mx = _modulation(c, ada_down[layer], ada_up[layer], ada_bias[layer])
        mc = _modulation(c_ctx[None], ada_down[layer], ada_up[layer], ada_bias[layer])
        hx = _modulate(_rmsnorm(zx, norm1_g[layer]), mx[:, 0], mx[:, 1])
        hc = _modulate(_rmsnorm(zc, norm1_g[layer]), mc[:, 0], mc[:, 1])
        if layer % 2 == 0:
            i = layer // 2
            yc, yx = _mixer_ab(hc, hx, w_in_ab[i], w_out_ab[i], qnorm_a[i], knorm_a[i],
                               sink_b[i], cos, sin, need_ctx)
        else:
            i = layer // 2
            yc, yx = _mixer_cd(hc, hx, w_in_cd[i], w_out_cd[i], rpb_c[i], dw_w[i], dw_b[i],
                               ln_d_g[i], ln_d_b[i], need_ctx)
        zx = zx + mx[:, 2][:, None, :] * yx
        hx2 = _modulate(_rmsnorm(zx, norm2_g[layer]), mx[:, 3], mx[:, 4])
        zx = zx + mx[:, 5][:, None, :] * _moe(hx2, router_group[layer], router_expert[layer],
                                                 w_gate[layer], w_up[layer], w_down[layer])
        if need_ctx:
            zc = zc + mc[:, 2][:, None, :] * yc
            hc2 = _modulate(_rmsnorm(zc, norm2_g[layer]), mc[:, 3], mc[:, 4])
            zc = zc + mc[:, 5][:, None, :] * _moe(hc2, router_group[layer], router_expert[layer],
                                                     w_gate[layer], w_up[layer], w_down[layer])
    return _rmsnorm(zx, final_g)
```

```python
import functools
import math

import numpy as np
import jax
import jax.numpy as jnp
from jax import lax
from jax.experimental import pallas as pl
from jax.experimental.pallas import tpu as pltpu

HEAD_DIM = 128
GRID_W = 64
ROPE_PAIRS = HEAD_DIM // 4
ROPE_THETA = 10000.0
WINDOW = 128
NA_KH = 8
NA_KW = 16
CONV_W = 31
N_MOD = 6
EXPERT_TOPK = 2
EPS = 1e-6
NEG = -1e30
LANES = 128
VMEM_LIMIT = 56 * 1024 * 1024
NA_ROWS = 8
CONV_HALO = 16

BF16 = jnp.bfloat16
F32 = jnp.float32


def _cparams(*sem):
    return pltpu.CompilerParams(dimension_semantics=sem, vmem_limit_bytes=VMEM_LIMIT)


def _row_tile(m, cap):
    best = None
    for t in range(LANES, min(m, cap) + 1, LANES):
        if m % t == 0:
            best = t
    assert best is not None, (m, cap)
    return best


def _sigmoid(x):
    return 1.0 / (1.0 + jnp.exp(-x))


def _mod_down_kernel(cond_ref, down_ref, o_ref):
    cnd = cond_ref[...]
    a = cnd * _sigmoid(cnd)
    o_ref[0] = jnp.dot(a, down_ref[0], preferred_element_type=F32, precision=lax.Precision.HIGHEST)


def _mod_up_kernel(t_ref, up_ref, b_ref, o_ref):
    o_ref[0] = jnp.dot(t_ref[0], up_ref[0], preferred_element_type=F32,
                       precision=lax.Precision.HIGHEST) + b_ref[0]


def _modulation_all(cond, ada_down, ada_up, ada_bias):
    depth, d, r = ada_down.shape
    n = ada_up.shape[2]
    t = pl.pallas_call(
        _mod_down_kernel,
        grid=(depth,),
        in_specs=[pl.BlockSpec((8, d), lambda l: (0, 0)),
                  pl.BlockSpec((1, d, r), lambda l: (l, 0, 0))],
        out_specs=pl.BlockSpec((1, 8, r), lambda l: (l, 0, 0)),
        out_shape=jax.ShapeDtypeStruct((depth, 8, r), F32),
        compiler_params=_cparams("arbitrary"),
        name="mod_down",
    )(cond, ada_down)
    tn = _row_tile(n, 2048)
    return pl.pallas_call(
        _mod_up_kernel,
        grid=(depth, n // tn),
        in_specs=[pl.BlockSpec((1, 8, r), lambda l, j: (l, 0, 0)),
                  pl.BlockSpec((1, r, tn), lambda l, j: (l, 0, j)),
                  pl.BlockSpec((1, 1, tn), lambda l, j: (l, 0, j))],
        out_specs=pl.BlockSpec((1, 8, tn), lambda l, j: (l, 0, j)),
        out_shape=jax.ShapeDtypeStruct((depth, 8, n), F32),
        compiler_params=_cparams("arbitrary", "arbitrary"),
        name="mod_up",
    )(t, ada_up, ada_bias.reshape(depth, 1, n))


def _norm_mod_kernel(n_lat, tm, modulated, z_ref, g_ref, *rest):
    if modulated:
        sh_l, sc_l, sh_c, sc_c, o_ref = rest
    else:
        (o_ref,) = rest
    x = z_ref[...]
    r = lax.rsqrt(jnp.mean(x * x, axis=-1, keepdims=True) + EPS)
    h = x * r * g_ref[...]
    if modulated:
        rows = pl.program_id(0) * tm + lax.broadcasted_iota(jnp.int32, (tm, 1), 0)
        is_lat = rows < n_lat
        shift = jnp.where(is_lat, sh_l[0], sh_c[0])
        scale = jnp.where(is_lat, sc_l[0], sc_c[0])
        h = h * (1.0 + scale) + shift
    o_ref[...] = h.astype(o_ref.dtype)


def _norm_mod(z, g, mods, a, b, rows, n_lat, out_dtype):
    d = z.shape[1]
    tm = _row_tile(rows, 256)
    modulated = mods is not None
    in_specs = [pl.BlockSpec((tm, d), lambda i: (i, 0)),
                pl.BlockSpec((1, d), lambda i: (0, 0))]
    args = [z, g.reshape(1, d)]
    if modulated:
        for idx in (a, b, N_MOD + a, N_MOD + b):
            in_specs.append(pl.BlockSpec((1, 1, d), lambda i, idx=idx: (idx, 0, 0)))
            args.append(mods)
    return pl.pallas_call(
        functools.partial(_norm_mod_kernel, n_lat, tm, modulated),
        grid=(rows // tm,),
        in_specs=in_specs,
        out_specs=pl.BlockSpec((tm, d), lambda i: (i, 0)),
        out_shape=jax.ShapeDtypeStruct((rows, d), out_dtype),
        compiler_params=_cparams("parallel"),
        name="norm_mod",
    )(*args)


def _rope_swap(x):
    lane = lax.broadcasted_iota(jnp.int32, x.shape, 1)
    first_half = (lane % (2 * ROPE_PAIRS)) < ROPE_PAIRS
    return jnp.where(first_half, pltpu.roll(x, HEAD_DIM - ROPE_PAIRS, 1), pltpu.roll(x, ROPE_PAIRS, 1))


def _proj_in_kernel(norm, rope, scale, heads, a_ref, w_ref, *rest):
    rest = list(rest)
    g_ref = rest.pop(0) if norm else None
    if rope:
        c_ref = rest.pop(0)
        s_ref = rest.pop(0)
    (o_ref,) = rest
    acc = jnp.dot(a_ref[...], w_ref[...], preferred_element_type=F32)
    if not (norm or rope):
        if scale != 1.0:
            acc = acc * scale
        o_ref[...] = acc.astype(o_ref.dtype)
        return
    for hh in range(heads):
        x = acc[:, hh * HEAD_DIM:(hh + 1) * HEAD_DIM]
        if norm:
            r = lax.rsqrt(jnp.mean(x * x, axis=-1, keepdims=True) + EPS)
            x = x * r * g_ref[...]
        if rope:
            x = x * c_ref[...] + _rope_swap(x) * s_ref[...]
        if scale != 1.0:
            x = x * scale
        o_ref[:, hh * HEAD_DIM:(hh + 1) * HEAD_DIM] = x.astype(o_ref.dtype)


def _proj_in(h, w, col0, width, rows, *, gain=None, rope_tabs=None, scale=1.0, out_dtype=BF16):
    k = h.shape[1]
    tm = _row_tile(rows, 1280)
    tn = math.gcd(math.gcd(width, 512), col0) if col0 else math.gcd(width, 512)
    norm = gain is not None
    rope = rope_tabs is not None
    in_specs = [pl.BlockSpec((tm, k), lambda j, i: (i, 0)),
                pl.BlockSpec((k, tn), lambda j, i: (0, col0 // tn + j))]
    args = [h, w]
    if norm:
        in_specs.append(pl.BlockSpec((1, HEAD_DIM), lambda j, i: (0, 0)))
        args.append(gain.reshape(1, HEAD_DIM))
    if rope:
        for tab in rope_tabs:
            in_specs.append(pl.BlockSpec((tm, HEAD_DIM), lambda j, i: (i, 0)))
            args.append(tab)
    return pl.pallas_call(
        functools.partial(_proj_in_kernel, norm, rope, scale, tn // HEAD_DIM),
        grid=(width // tn, rows // tm),
        in_specs=in_specs,
        out_specs=pl.BlockSpec((tm, tn), lambda j, i: (i, j)),
        out_shape=jax.ShapeDtypeStruct((rows, width), out_dtype),
        compiler_params=_cparams("parallel", "parallel"),
        name="proj_in",
    )(*args)


def _proj_glu_kernel(a_ref, wa_ref, wg_ref, o_ref):
    a = a_ref[...]
    va = jnp.dot(a, wa_ref[...], preferred_element_type=F32)
    vg = jnp.dot(a, wg_ref[...], preferred_element_type=F32)
    o_ref[...] = (va * _sigmoid(vg)).astype(o_ref.dtype)


def _proj_glu(h, w, col0, width, rows):
    k = h.shape[1]
    tm = _row_tile(rows, 1280)
    tn = math.gcd(math.gcd(width, 512), col0)
    return pl.pallas_call(
        _proj_glu_kernel,
        grid=(width // tn, rows // tm),
        in_specs=[pl.BlockSpec((tm, k), lambda j, i: (i, 0)),
                  pl.BlockSpec((k, tn), lambda j, i: (0, col0 // tn + j)),
                  pl.BlockSpec((k, tn), lambda j, i: (0, (col0 + width) // tn + j))],
        out_specs=pl.BlockSpec((tm, tn), lambda j, i: (i, j)),
        out_shape=jax.ShapeDtypeStruct((rows, width), F32),
        compiler_params=_cparams("parallel", "parallel"),
        name="proj_glu",
    )(h, w, w)


def _proj_out_kernel(n_lat, tm, a1_ref, a2_ref, w1_ref, w2_ref, z_ref, gl_ref, gc_ref, o_ref):
    acc = jnp.dot(a1_ref[...], w1_ref[...], preferred_element_type=F32)
    acc = acc + jnp.dot(a2_ref[...], w2_ref[...], preferred_element_type=F32)
    rows = pl.program_id(1) * tm + lax.broadcasted_iota(jnp.int32, (tm, 1), 0)
    gate = jnp.where(rows < n_lat, gl_ref[0], gc_ref[0])
    o_ref[...] = z_ref[...] + gate * acc


def _proj_out(a1, a2, w, z, mods, gate_idx, rows, n_lat):
    k1, k2 = a1.shape[1], a2.shape[1]
    assert k1 == k2
    n = w.shape[1]
    tm = _row_tile(rows, 1280)
    tn = math.gcd(n, 512)
    return pl.pallas_call(
        functools.partial(_proj_out_kernel, n_lat, tm),
        grid=(n // tn, rows // tm),
        in_specs=[pl.BlockSpec((tm, k1), lambda j, i: (i, 0)),
                  pl.BlockSpec((tm, k2), lambda j, i: (i, 0)),
                  pl.BlockSpec((k1, tn), lambda j, i: (0, j)),
                  pl.BlockSpec((k2, tn), lambda j, i: (1, j)),
                  pl.BlockSpec((tm, tn), lambda j, i: (i, j)),
                  pl.BlockSpec((1, 1, tn), lambda j, i: (gate_idx, 0, j)),
                  pl.BlockSpec((1, 1, tn), lambda j, i: (N_MOD + gate_idx, 0, j))],
        out_specs=pl.BlockSpec((tm, tn), lambda j, i: (i, j)),
        out_shape=jax.ShapeDtypeStruct((rows, n), F32),
        compiler_params=_cparams("parallel", "parallel"),
        name="proj_out",
    )(a1, a2, w, w, z, mods, mods)


def _nt_dot(a, b):
    return lax.dot_general(a, b, (((1,), (1,)), ((), ())), preferred_element_type=F32)


def _online_update(g, s, v, m_ref, l_ref, acc_ref):
    m_prev = m_ref[g]
    m_new = jnp.maximum(m_prev, jnp.max(s, axis=-1, keepdims=True))
    alpha = jnp.exp(m_prev - m_new)
    p = jnp.exp(s - m_new)
    l_ref[g] = alpha * l_ref[g] + jnp.sum(p, axis=-1, keepdims=True)
    acc_ref[g] = alpha * acc_ref[g] + jnp.dot(p.astype(v.dtype), v, preferred_element_type=F32)
    m_ref[g] = m_new


def _global_attn_kernel(groups, q_ref, k_ref, v_ref, kc_ref, vc_ref, o_ref, m_ref, l_ref, acc_ref):
    j = pl.program_id(2)

    @pl.when(j == 0)
    def _():
        m_ref[...] = jnp.full(m_ref.shape, NEG, F32)
        l_ref[...] = jnp.zeros(l_ref.shape, F32)
        acc_ref[...] = jnp.zeros(acc_ref.shape, F32)
        for g in range(groups):
            q = q_ref[:, g * HEAD_DIM:(g + 1) * HEAD_DIM]
            _online_update(g, _nt_dot(q, kc_ref[...]), vc_ref[...], m_ref, l_ref, acc_ref)

    for g in range(groups):
        q = q_ref[:, g * HEAD_DIM:(g + 1) * HEAD_DIM]
        _online_update(g, _nt_dot(q, k_ref[...]), v_ref[...], m_ref, l_ref, acc_ref)

    @pl.when(j == pl.num_programs(2) - 1)
    def _():
        for g in range(groups):
            o_ref[:, g * HEAD_DIM:(g + 1) * HEAD_DIM] = (acc_ref[g] / l_ref[g]).astype(o_ref.dtype)


def _global_attn(q, k, v, n_lat, n_ctx, out_rows):
    hq = q.shape[1] // HEAD_DIM
    hkv = k.shape[1] // HEAD_DIM
    groups = hq // hkv
    tq = _row_tile(n_lat, 256)
    tk = _row_tile(n_lat, 512)
    gw = groups * HEAD_DIM
    ctx_blk = n_lat // n_ctx
    return pl.pallas_call(
        functools.partial(_global_attn_kernel, groups),
        grid=(hkv, n_lat // tq, n_lat // tk),
        in_specs=[pl.BlockSpec((tq, gw), lambda h, i, j: (i, h)),
                  pl.BlockSpec((tk, HEAD_DIM), lambda h, i, j: (j, h)),
                  pl.BlockSpec((tk, HEAD_DIM), lambda h, i, j: (j, h)),
                  pl.BlockSpec((n_ctx, HEAD_DIM), lambda h, i, j: (ctx_blk, h)),
                  pl.BlockSpec((n_ctx, HEAD_DIM), lambda h, i, j: (ctx_blk, h))],
        out_specs=pl.BlockSpec((tq, gw), lambda h, i, j: (i, h)),
        out_shape=jax.ShapeDtypeStruct((out_rows, q.shape[1]), BF16),
        scratch_shapes=[pltpu.VMEM((groups, tq, 1), F32),
                        pltpu.VMEM((groups, tq, 1), F32),
                        pltpu.VMEM((groups, tq, HEAD_DIM), F32)],
        compiler_params=_cparams("parallel", "parallel", "arbitrary"),
        name="global_attn",
    )(q, k, v, k, v)


def _window_attn_kernel(groups, tq, n_lat, sink_ref, q_ref, k_ref, v_ref, kc_ref, vc_ref, o_ref,
                        m_ref, l_ref, acc_ref):
    h = pl.program_id(0)
    i = pl.program_id(1)
    s_idx = pl.program_id(2)

    @pl.when(s_idx == 0)
    def _():
        for g in range(groups):
            m_ref[g] = jnp.full((tq, 1), sink_ref[h * groups + g], F32)
        l_ref[...] = jnp.ones(l_ref.shape, F32)
        acc_ref[...] = jnp.zeros(acc_ref.shape, F32)
        for g in range(groups):
            q = q_ref[:, g * HEAD_DIM:(g + 1) * HEAD_DIM]
            _online_update(g, _nt_dot(q, kc_ref[...]), vc_ref[...], m_ref, l_ref, acc_ref)

    qpos = i * tq + lax.broadcasted_iota(jnp.int32, (tq, tq), 0)
    kpos = (i + s_idx - 1) * tq + lax.broadcasted_iota(jnp.int32, (tq, tq), 1)
    ok = (jnp.abs(qpos - kpos) <= WINDOW) & (kpos >= 0) & (kpos < n_lat)
    for g in range(groups):
        q = q_ref[:, g * HEAD_DIM:(g + 1) * HEAD_DIM]
        s = jnp.where(ok, _nt_dot(q, k_ref[...]), NEG)
        _online_update(g, s, v_ref[...], m_ref, l_ref, acc_ref)

    @pl.when(s_idx == pl.num_programs(2) - 1)
    def _():
        for g in range(groups):
            o_ref[:, g * HEAD_DIM:(g + 1) * HEAD_DIM] = (acc_ref[g] / l_ref[g]).astype(o_ref.dtype)


def _window_attn(q, k, v, sink, n_lat, n_ctx, out_rows):
    hq = q.shape[1] // HEAD_DIM
    hkv = k.shape[1] // HEAD_DIM
    groups = hq // hkv
    tq = _row_tile(n_lat, 256)
    assert tq >= WINDOW
    nb = n_lat // tq
    gw = groups * HEAD_DIM
    ctx_blk = n_lat // n_ctx

    def kv_map(h, i, s):
        return (jnp.clip(i + s - 1, 0, nb - 1), h)

    return pl.pallas_call(
        functools.partial(_window_attn_kernel, groups, tq, n_lat),
        grid=(hkv, nb, 3),
        in_specs=[pl.BlockSpec(memory_space=pltpu.SMEM),
                  pl.BlockSpec((tq, gw), lambda h, i, s: (i, h)),
                  pl.BlockSpec((tq, HEAD_DIM), kv_map),
                  pl.BlockSpec((tq, HEAD_DIM), kv_map),
                  pl.BlockSpec((n_ctx, HEAD_DIM), lambda h, i, s: (ctx_blk, h)),
                  pl.BlockSpec((n_ctx, HEAD_DIM), lambda h, i, s: (ctx_blk, h))],
        out_specs=pl.BlockSpec((tq, gw), lambda h, i, s: (i, h)),
        out_shape=jax.ShapeDtypeStruct((out_rows, q.shape[1]), BF16),
        scratch_shapes=[pltpu.VMEM((groups, tq, 1), F32),
                        pltpu.VMEM((groups, tq, 1), F32),
                        pltpu.VMEM((groups, tq, HEAD_DIM), F32)],
        compiler_params=_cparams("parallel", "parallel", "arbitrary"),
        name="window_attn",
    )(sink, q, k, v, k, v)


def _ctx_attn_kernel(groups, has_sink, *refs):
    if has_sink:
        sink_ref, q_ref, k_ref, v_ref, o_ref = refs
    else:
        q_ref, k_ref, v_ref, o_ref = refs
    h = pl.program_id(0)
    for g in range(groups):
        q = q_ref[:, g * HEAD_DIM:(g + 1) * HEAD_DIM]
        s = _nt_dot(q, k_ref[...])
        m = jnp.max(s, axis=-1, keepdims=True)
        if has_sink:
            snk = sink_ref[h * groups + g]
            m = jnp.maximum(m, snk)
        p = jnp.exp(s - m)
        l = jnp.sum(p, axis=-1, keepdims=True)
        if has_sink:
            l = l + jnp.exp(snk - m)
        o = jnp.dot(p.astype(v_ref.dtype), v_ref[...], preferred_element_type=F32)
        o_ref[:, g * HEAD_DIM:(g + 1) * HEAD_DIM] = (o / l).astype(o_ref.dtype)


def _ctx_attn(q, k, v, sink, n_lat, n_ctx):
    hq = q.shape[1] // HEAD_DIM
    hkv = k.shape[1] // HEAD_DIM
    groups = hq // hkv
    gw = groups * HEAD_DIM
    ctx_blk = n_lat // n_ctx
    has_sink = sink is not None
    in_specs = [pl.BlockSpec((n_ctx, gw), lambda h: (ctx_blk, h)),
                pl.BlockSpec((n_ctx, HEAD_DIM), lambda h: (ctx_blk, h)),
                pl.BlockSpec((n_ctx, HEAD_DIM), lambda h: (ctx_blk, h))]
    args = [q, k, v]
    if has_sink:
        in_specs.insert(0, pl.BlockSpec(memory_space=pltpu.SMEM))
        args.insert(0, sink)
    return pl.pallas_call(
        functools.partial(_ctx_attn_kernel, groups, has_sink),
        grid=(hkv,),
        in_specs=in_specs,
        out_specs=pl.BlockSpec((n_ctx, gw), lambda h: (0, h)),
        out_shape=jax.ShapeDtypeStruct((n_ctx, q.shape[1]), BF16),
        compiler_params=_cparams("parallel"),
        name="ctx_attn",
    )(*args)


def _na_bias_table(rpb, n_rows):
    kh = min(NA_KH, n_rows)
    assert kh == NA_KH and n_rows % NA_ROWS == 0 and n_rows // NA_ROWS >= 2
    qr = np.arange(NA_ROWS)
    krel = np.arange(3 * NA_ROWS) - NA_ROWS
    ri = np.zeros((3, NA_ROWS, 3 * NA_ROWS), np.int32)
    rvalid = np.zeros((3, NA_ROWS, 3 * NA_ROWS), bool)
    big = 4 * NA_ROWS
    for var, (r0, rows) in enumerate(((0, big), (NA_ROWS, big), (n_rows - NA_ROWS, n_rows))):
        r = r0 + qr[:, None]
        key = r0 + krel[None, :]
        rs = np.clip(r - kh // 2, 0, rows - kh)
        rvalid[var] = (key >= rs) & (key < rs + kh)
        ri[var] = np.clip(key - r + NA_KH - 1, 0, 2 * NA_KH - 2)
    qc = np.arange(GRID_W)
    cs = np.clip(qc - NA_KW // 2, 0, GRID_W - NA_KW)
    kc = np.arange(GRID_W)
    cvalid = (kc[None, :] >= cs[:, None]) & (kc[None, :] < cs[:, None] + NA_KW)
    ci = np.clip(kc[None, :] - qc[:, None] + NA_KW - 1, 0, 2 * NA_KW - 2)
    t = jnp.take(rpb, jnp.asarray(ci), axis=2)
    t = jnp.take(t, jnp.asarray(ri), axis=1)
    valid = rvalid[:, :, :, None, None] & cvalid[None, None, None, :, :]
    t = jnp.where(jnp.asarray(valid)[None], t, NEG)
    t = t.transpose(0, 1, 2, 4, 3, 5)
    nq = NA_ROWS * GRID_W
    return t.reshape(rpb.shape[0], 3, nq, 3 * nq)


def _na_kernel(nq, q_ref, kp_ref, k0_ref, kn_ref, vp_ref, v0_ref, vn_ref, kc_ref, vc_ref, b_ref, o_ref):
    q = q_ref[...]
    parts = []
    for idx, k_ref in enumerate((kp_ref, k0_ref, kn_ref)):
        parts.append(_nt_dot(q, k_ref[...]) + b_ref[0, 0, :, idx * nq:(idx + 1) * nq])
    parts.append(_nt_dot(q, kc_ref[...]))
    m = parts[0].max(axis=-1, keepdims=True)
    for s in parts[1:]:
        m = jnp.maximum(m, s.max(axis=-1, keepdims=True))
    l = jnp.zeros_like(m)
    o = jnp.zeros((nq, HEAD_DIM), F32)
    for s, v_ref in zip(parts, (vp_ref, v0_ref, vn_ref, vc_ref)):
        p = jnp.exp(s - m)
        l = l + jnp.sum(p, axis=-1, keepdims=True)
        o = o + jnp.dot(p.astype(v_ref.dtype), v_ref[...], preferred_element_type=F32)
    o_ref[...] = (o / l).astype(o_ref.dtype)


def _na_attn(q, k, v, bias, n_lat, n_ctx, out_rows):
    heads = q.shape[1] // HEAD_DIM
    nq = NA_ROWS * GRID_W
    nb = n_lat // nq
    ctx_blk = n_lat // n_ctx

    def prev_map(h, b):
        return (jnp.maximum(b - 1, 0), h)

    def next_map(h, b):
        return (jnp.minimum(b + 1, nb - 1), h)

    def own_map(h, b):
        return (b, h)

    def ctx_map(h, b):
        return (ctx_blk, h)

    def bias_map(h, b):
        return (h, jnp.where(b == 0, 0, jnp.where(b == nb - 1, 2, 1)), 0, 0)

    blk = (nq, HEAD_DIM)
    return pl.pallas_call(
        functools.partial(_na_kernel, nq),
        grid=(heads, nb),
        in_specs=[pl.BlockSpec(blk, own_map),
                  pl.BlockSpec(blk, prev_map), pl.BlockSpec(blk, own_map), pl.BlockSpec(blk, next_map),
                  pl.BlockSpec(blk, prev_map), pl.BlockSpec(blk, own_map), pl.BlockSpec(blk, next_map),
                  pl.BlockSpec((n_ctx, HEAD_DIM), ctx_map), pl.BlockSpec((n_ctx, HEAD_DIM), ctx_map),
                  pl.BlockSpec((1, 1, nq, 3 * nq), bias_map)],
        out_specs=pl.BlockSpec(blk, own_map),
        out_shape=jax.ShapeDtypeStruct((out_rows, q.shape[1]), BF16),
        compiler_params=_cparams("parallel", "arbitrary"),
        name="na_attn",
    )(q, k, k, k, v, v, v, k, v, bias)


def _conv_kernel(tr, n_lat_tiles, n_tiles, ch_chunk, prev_ref, cur_ref, next_ref, w_ref, b_ref,
                 g_ref, beta_ref, o_ref, win_ref, y_ref):
    i = pl.program_id(0)
    c = cur_ref.shape[1]
    prev_ok = jnp.logical_and(i != 0, i != n_lat_tiles)
    next_ok = jnp.logical_and(i != n_lat_tiles - 1, i != n_tiles - 1)
    win_ref[0:CONV_HALO, :] = jnp.where(prev_ok, prev_ref[...], 0.0)
    win_ref[CONV_HALO:CONV_HALO + tr, :] = cur_ref[...]
    win_ref[CONV_HALO + tr:, :] = jnp.where(next_ok, next_ref[...], 0.0)
    off = CONV_HALO - CONV_W // 2
    rc = 64

    def col_body(cc, carry):
        c0 = pl.multiple_of(cc * ch_chunk, ch_chunk)
        for r0 in range(0, tr, rc):
            acc = jnp.zeros((rc, ch_chunk), F32)
            for t in range(CONV_W):
                acc = acc + win_ref[r0 + off + t:r0 + off + t + rc, pl.ds(c0, ch_chunk)] * \
                    w_ref[t:t + 1, pl.ds(c0, ch_chunk)]
            y_ref[r0:r0 + rc, pl.ds(c0, ch_chunk)] = acc + b_ref[:, pl.ds(c0, ch_chunk)]
        return carry

    lax.fori_loop(0, c // ch_chunk, col_body, 0)
    y = y_ref[...]
    mu = jnp.mean(y, axis=-1, keepdims=True)
    var = jnp.mean(jnp.square(y - mu), axis=-1, keepdims=True)
    yn = (y - mu) * lax.rsqrt(var + EPS) * g_ref[...] + beta_ref[...]
    o_ref[...] = (yn * _sigmoid(yn)).astype(o_ref.dtype)


def _conformer_conv(u, dw_w, dw_b, ln_g, ln_b, rows, n_lat):
    c = u.shape[1]
    tr = 256
    assert n_lat % tr == 0 and rows % tr == 0
    hb = tr // CONV_HALO
    n_tiles = rows // tr
    last_halo = rows // CONV_HALO - 1
    ch_chunk = math.gcd(c, 256)
    return pl.pallas_call(
        functools.partial(_conv_kernel, tr, n_lat // tr, n_tiles, ch_chunk),
        grid=(n_tiles,),
        in_specs=[pl.BlockSpec((CONV_HALO, c), lambda i: (jnp.maximum(i * hb - 1, 0), 0)),
                  pl.BlockSpec((tr, c), lambda i: (i, 0)),
                  pl.BlockSpec((CONV_HALO, c), lambda i: (jnp.minimum((i + 1) * hb, last_halo), 0)),
                  pl.BlockSpec((CONV_W, c), lambda i: (0, 0)),
                  pl.BlockSpec((1, c), lambda i: (0, 0)),
                  pl.BlockSpec((1, c), lambda i: (0, 0)),
                  pl.BlockSpec((1, c), lambda i: (0, 0))],
        out_specs=pl.BlockSpec((tr, c), lambda i: (i, 0)),
        out_shape=jax.ShapeDtypeStruct((rows, c), BF16),
        scratch_shapes=[pltpu.VMEM((tr + 2 * CONV_HALO, c), F32),
                        pltpu.VMEM((tr, c), F32)],
        compiler_params=_cparams("parallel"),
        name="conformer_conv",
    )(u, u, u, dw_w, dw_b.reshape(1, c), ln_g.reshape(1, c), ln_b.reshape(1, c))


def _router_kernel(n_groups, per_group, h_ref, w_ref, eid_ref, wt_ref):
    lg = jnp.dot(h_ref[...], w_ref[...], preferred_element_type=F32)
    lane = lax.broadcasted_iota(jnp.int32, lg.shape, 1).astype(F32)
    is_g = lane < n_groups
    gl = jnp.where(is_g, lg, NEG)
    mg = jnp.max(gl, axis=-1, keepdims=True)
    gsel = jnp.min(jnp.where(gl == mg, lane, float(LANES)), axis=-1, keepdims=True)
    gsum = jnp.sum(jnp.where(is_g, jnp.exp(gl - mg), 0.0), axis=-1, keepdims=True)
    lo = n_groups + gsel * per_group
    el = jnp.where((lane >= lo) & (lane < lo + per_group), lg, NEG)
    v1 = jnp.max(el, axis=-1, keepdims=True)
    i1 = jnp.min(jnp.where(el == v1, lane, float(LANES)), axis=-1, keepdims=True)
    el2 = jnp.where(lane == i1, NEG, el)
    v2 = jnp.max(el2, axis=-1, keepdims=True)
    i2 = jnp.min(jnp.where(el2 == v2, lane, float(LANES)), axis=-1, keepdims=True)
    e2 = jnp.exp(v2 - v1)
    w1 = 1.0 / ((1.0 + e2) * gsum)
    w2 = e2 * w1
    eid = jnp.where(lane == 0, i1 - n_groups, jnp.where(lane == 1, i2 - n_groups, 0.0))
    eid_ref[...] = eid.astype(jnp.int32)
    wt_ref[...] = jnp.where(lane == 0, w1, jnp.where(lane == 1, w2, 0.0))


def _router(h, w_r, n_groups, per_group, rows):
    d = h.shape[1]
    tm = _row_tile(rows, 512)
    return pl.pallas_call(
        functools.partial(_router_kernel, n_groups, per_group),
        grid=(rows // tm,),
        in_specs=[pl.BlockSpec((tm, d), lambda i: (i, 0)),
                  pl.BlockSpec((d, LANES), lambda i: (0, 0))],
        out_specs=[pl.BlockSpec((tm, LANES), lambda i: (i, 0)),
                   pl.BlockSpec((tm, LANES), lambda i: (i, 0))],
        out_shape=[jax.ShapeDtypeStruct((rows, LANES), jnp.int32),
                   jax.ShapeDtypeStruct((rows, LANES), F32)],
        compiler_params=_cparams("parallel"),
        name="moe_router",
    )(h, w_r)


def _expert_kernel(te_ref, tv_ref, x_ref, wt_ref, wg_ref, wu_ref, wd_ref, o_ref):
    i = pl.program_id(0)

    @pl.when(tv_ref[i] > 0)
    def _():
        x = x_ref[...]
        hg = jnp.dot(x, wg_ref[0], preferred_element_type=F32)
        hu = jnp.dot(x, wu_ref[0], preferred_element_type=F32)
        act = (hg * _sigmoid(hg)) * hu * wt_ref[...]
        o_ref[...] = jnp.dot(act.astype(wd_ref.dtype), wd_ref[0], preferred_element_type=F32)

    @pl.when(tv_ref[i] == 0)
    def _():
        o_ref[...] = jnp.zeros(o_ref.shape, o_ref.dtype)


def _expert_ffn(xs, row_w, tile_e, tile_valid, w_g, w_u, w_d, tm):
    p, d = xs.shape
    f = w_g.shape[2]
    grid_spec = pltpu.PrefetchScalarGridSpec(
        num_scalar_prefetch=2,
        grid=(p // tm,),
        in_specs=[pl.BlockSpec((tm, d), lambda i, te, tv: (i, 0)),
                  pl.BlockSpec((tm, 1), lambda i, te, tv: (i, 0)),
                  pl.BlockSpec((1, d, f), lambda i, te, tv: (te[i], 0, 0)),
                  pl.BlockSpec((1, d, f), lambda i, te, tv: (te[i], 0, 0)),
                  pl.BlockSpec((1, f, d), lambda i, te, tv: (te[i], 0, 0))],
        out_specs=pl.BlockSpec((tm, d), lambda i, te, tv: (i, 0)),
    )
    return pl.pallas_call(
        _expert_kernel,
        grid_spec=grid_spec,
        out_shape=jax.ShapeDtypeStruct((p, d), F32),
        compiler_params=_cparams("arbitrary"),
        name="moe_experts",
    )(tile_e, tile_valid, xs, row_w, w_g, w_u, w_d)


def _combine_kernel(n_lat, tm, z_ref, ya_ref, yb_ref, gl_ref, gc_ref, o_ref):
    rows = pl.program_id(0) * tm + lax.broadcasted_iota(jnp.int32, (tm, 1), 0)
    gate = jnp.where(rows < n_lat, gl_ref[0], gc_ref[0])
    o_ref[...] = z_ref[...] + gate * (ya_ref[...] + yb_ref[...])


def _combine(z, ya, yb, mods, gate_idx, rows, n_lat):
    d = z.shape[1]
    tm = _row_tile(rows, 256)
    return pl.pallas_call(
        functools.partial(_combine_kernel, n_lat, tm),
        grid=(rows // tm,),
        in_specs=[pl.BlockSpec((tm, d), lambda i: (i, 0)),
                  pl.BlockSpec((tm, d), lambda i: (i, 0)),
                  pl.BlockSpec((tm, d), lambda i: (i, 0)),
                  pl.BlockSpec((1, 1, d), lambda i: (gate_idx, 0, 0)),
                  pl.BlockSpec((1, 1, d), lambda i: (N_MOD + gate_idx, 0, 0))],
        out_specs=pl.BlockSpec((tm, d), lambda i: (i, 0)),
        out_shape=jax.ShapeDtypeStruct((rows, d), F32),
        compiler_params=_cparams("parallel"),
        name="moe_combine",
    )(z, ya, yb, mods, mods)


def _moe(h2, z, mods, w_r, w_g, w_u, w_d, n_groups, per_group, rows, n_lat):
    n_exp = n_groups * per_group
    tm = 256
    eid_l, wt_l = _router(h2, w_r, n_groups, per_group, rows)
    flat_e = eid_l[:, :EXPERT_TOPK].reshape(-1)
    flat_w = wt_l[:, :EXPERT_TOPK].reshape(-1)
    n_pairs = rows * EXPERT_TOPK
    onehot = (flat_e[:, None] == jnp.arange(n_exp, dtype=jnp.int32)[None, :]).astype(jnp.int32)
    csum = jnp.cumsum(onehot, axis=0)
    rank = jnp.sum((csum - onehot) * onehot, axis=1)
    counts = csum[-1]
    padded = ((counts + tm - 1) // tm) * tm
    ends = jnp.cumsum(padded)
    starts = ends - padded
    pos = starts[flat_e] + rank
    n_tiles = (n_pairs + n_exp * (tm - 1)) // tm + 1
    p = n_tiles * tm
    row_token = jnp.zeros((p,), jnp.int32).at[pos].set(jnp.arange(n_pairs, dtype=jnp.int32) // EXPERT_TOPK)
    row_w = jnp.zeros((p,), F32).at[pos].set(flat_w)
    tile_start = jnp.arange(n_tiles, dtype=jnp.int32) * tm
    tile_e = jnp.minimum(jnp.searchsorted(ends, tile_start, side="right"), n_exp - 1).astype(jnp.int32)
    tile_valid = (tile_start < ends[-1]).astype(jnp.int32)
    xs = jnp.take(h2, row_token, axis=0)
    ys = _expert_ffn(xs, row_w.reshape(p, 1), tile_e, tile_valid, w_g, w_u, w_d, tm)
    pos2 = pos.reshape(rows, EXPERT_TOPK)
    ya = jnp.take(ys, pos2[:, 0], axis=0)
    yb = jnp.take(ys, pos2[:, 1], axis=0)
    return _combine(z, ya, yb, mods, 5, rows, n_lat)


def _rope_tables(n_lat, n_ctx):
    t = jnp.arange(n_lat)
    row = (t // GRID_W).astype(F32)
    col = (t % GRID_W).astype(F32)
    inv = ROPE_THETA ** (-jnp.arange(ROPE_PAIRS, dtype=F32) / ROPE_PAIRS)
    ar = row[:, None] * inv
    ac = col[:, None] * inv
    cos = jnp.concatenate([jnp.cos(ar), jnp.cos(ar), jnp.cos(ac), jnp.cos(ac)], axis=1)
    sin = jnp.concatenate([-jnp.sin(ar), jnp.sin(ar), -jnp.sin(ac), jnp.sin(ac)], axis=1)
    cos = jnp.concatenate([cos, jnp.ones((n_ctx, HEAD_DIM), F32)], axis=0)
    sin = jnp.concatenate([sin, jnp.zeros((n_ctx, HEAD_DIM), F32)], axis=0)
    return cos, sin


def _with_ctx(o_lat, o_ctx, n_lat):
    return lax.dynamic_update_slice(o_lat, o_ctx, (n_lat, 0))


def kernel(x, c, ctx, c_ctx, ada_down, ada_up, ada_bias, norm1_g, norm2_g, w_in_ab, w_out_ab, qnorm_a,
           knorm_a, sink_b, w_in_cd, w_out_cd, rpb_c, dw_w, dw_b, ln_d_g, ln_d_b, router_group,
           router_expert, w_gate, w_up, w_down, final_g):
    assert x.shape[0] == 1 and ctx.shape[0] == 1
    n_lat, d = x.shape[1], x.shape[2]
    n_ctx = ctx.shape[1]
    n_all = n_lat + n_ctx
    depth = ada_down.shape[0]
    assert n_lat % n_ctx == 0 and n_ctx % 128 == 0
    n_groups, per_group = router_expert.shape[1], router_expert.shape[3]
    n_mix = d // HEAD_DIM
    a_w = (n_mix // 2) * HEAD_DIM
    akv_w = (n_mix // 8) * HEAD_DIM
    c_w = (n_mix // 2) * HEAD_DIM
    d_ch = d // 2
    scale = HEAD_DIM ** -0.5

    z = jnp.concatenate([x[0], ctx[0]], axis=0)
    cond = jnp.zeros((8, d), F32).at[0].set(c[0]).at[1].set(c_ctx)
    mods_all = _modulation_all(cond, ada_down, ada_up, ada_bias)
    tabs = _rope_tables(n_lat, n_ctx)

    for layer in range(depth):
        need_ctx = layer < depth - 1
        rows = n_all if need_ctx else n_lat
        mods = mods_all[layer, :2].reshape(2 * N_MOD, 1, d)
        h = _norm_mod(z, norm1_g[layer], mods, 0, 1, n_all, n_lat, BF16)
        i = layer // 2
        if layer % 2 == 0:
            w = w_in_ab[i].astype(BF16)
            col = 0
            qa = _proj_in(h, w, col, a_w, n_all, gain=qnorm_a[i], rope_tabs=tabs, scale=scale)
            col += a_w
            ka = _proj_in(h, w, col, akv_w, n_all, gain=knorm_a[i], rope_tabs=tabs)
            col += akv_w
            va = _proj_in(h, w, col, akv_w, n_all)
            col += akv_w
            qb = _proj_in(h, w, col, a_w, n_all, rope_tabs=tabs, scale=scale)
            col += a_w
            kb = _proj_in(h, w, col, akv_w, n_all, rope_tabs=tabs)
            col += akv_w
            vb = _proj_in(h, w, col, akv_w, n_all)
            o1 = _global_attn(qa, ka, va, n_lat, n_ctx, rows)
            o2 = _window_attn(qb, kb, vb, sink_b[i], n_lat, n_ctx, rows)
            if need_ctx:
                o1 = _with_ctx(o1, _ctx_attn(qa, ka, va, None, n_lat, n_ctx), n_lat)
                o2 = _with_ctx(o2, _ctx_attn(qb, kb, vb, sink_b[i], n_lat, n_ctx), n_lat)
            w_out = w_out_ab[i].astype(BF16)
        else:
            w = w_in_cd[i].astype(BF16)
            qc = _proj_in(h, w, 0, c_w, n_all, scale=scale)
            kc = _proj_in(h, w, c_w, c_w, n_all)
            vc = _proj_in(h, w, 2 * c_w, c_w, n_all)
            u = _proj_glu(h, w, 3 * c_w, d_ch, n_all)
            bias = _na_bias_table(rpb_c[i], n_lat // GRID_W)
            o1 = _na_attn(qc, kc, vc, bias, n_lat, n_ctx, rows)
            if need_ctx:
                o1 = _with_ctx(o1, _ctx_attn(qc, kc, vc, None, n_lat, n_ctx), n_lat)
            o2 = _conformer_conv(u, dw_w[i], dw_b[i], ln_d_g[i], ln_d_b[i], rows, n_lat)
            w_out = w_out_cd[i].astype(BF16)
        z = _proj_out(o1, o2, w_out, z, mods, 2, rows, n_lat)
        h2 = _norm_mod(z, norm2_g[layer], mods, 3, 4, rows, n_lat, BF16)
        w_r = jnp.concatenate(
            [router_group[layer], router_expert[layer].transpose(1, 0, 2).reshape(d, n_groups * per_group)],
            axis=1)
        w_r = jnp.pad(w_r, ((0, 0), (0, LANES - w_r.shape[1]))).astype(BF16)
        z = _moe(h2, z, mods, w_r, w_gate[layer].astype(BF16), w_up[layer].astype(BF16),
                 w_down[layer].astype(BF16), n_groups, per_group, rows, n_lat)
    out = _norm_mod(z, final_g, None, 0, 0, n_lat, n_lat, F32)
    return out[None]
```

```python
import functools
import math

import numpy as np
import jax
import jax.numpy as jnp
from jax import lax
from jax.experimental import pallas as pl
from jax.experimental.pallas import tpu as pltpu

HEAD_DIM = 128
GRID_W = 64
ROPE_PAIRS = HEAD_DIM // 4
ROPE_THETA = 10000.0
WINDOW = 128
NA_KH = 8
NA_KW = 16
CONV_W = 31
N_MOD = 6
EXPERT_TOPK = 2
EPS = 1e-6
NEG = -1e30
LOG2E = math.log2(math.e)
LANES = 128
VMEM_LIMIT = 56 * 1024 * 1024
NA_ROWS = 8
CONV_HALO = 16
EXT = 2 * HEAD_DIM
MOE_TILE = 256

BF16 = jnp.bfloat16
F32 = jnp.float32


def _cparams(*sem):
    return pltpu.CompilerParams(dimension_semantics=sem, vmem_limit_bytes=VMEM_LIMIT)


def _row_tile(m, cap):
    best = None
    for t in range(LANES, min(m, cap) + 1, LANES):
        if m % t == 0:
            best = t
    assert best is not None, (m, cap)
    return best


def _sigmoid(x):
    return 1.0 / (1.0 + jnp.exp(-x))


def _mod_down_kernel(cond_ref, down_ref, o_ref):
    cnd = cond_ref[...]
    a = cnd * _sigmoid(cnd)
    o_ref[0] = jnp.dot(a, down_ref[0], preferred_element_type=F32, precision=lax.Precision.HIGHEST)


def _mod_up_kernel(t_ref, up_ref, b_ref, o_ref):
    o_ref[0] = jnp.dot(t_ref[0], up_ref[0], preferred_element_type=F32,
                       precision=lax.Precision.HIGHEST) + b_ref[0]


def _modulation_all(cond, ada_down, ada_up, ada_bias):
    depth, d, r = ada_down.shape
    n = ada_up.shape[2]
    t = pl.pallas_call(
        _mod_down_kernel,
        grid=(depth,),
        in_specs=[pl.BlockSpec((8, d), lambda l: (0, 0)),
                  pl.BlockSpec((1, d, r), lambda l: (l, 0, 0))],
        out_specs=pl.BlockSpec((1, 8, r), lambda l: (l, 0, 0)),
        out_shape=jax.ShapeDtypeStruct((depth, 8, r), F32),
        compiler_params=_cparams("arbitrary"),
        name="mod_down",
    )(cond, ada_down)
    tn = _row_tile(n, 2048)
    return pl.pallas_call(
        _mod_up_kernel,
        grid=(depth, n // tn),
        in_specs=[pl.BlockSpec((1, 8, r), lambda l, j: (l, 0, 0)),
                  pl.BlockSpec((1, r, tn), lambda l, j: (l, 0, j)),
                  pl.BlockSpec((1, 1, tn), lambda l, j: (l, 0, j))],
        out_specs=pl.BlockSpec((1, 8, tn), lambda l, j: (l, 0, j)),
        out_shape=jax.ShapeDtypeStruct((depth, 8, n), F32),
        compiler_params=_cparams("arbitrary", "arbitrary"),
        name="mod_up",
    )(t, ada_up, ada_bias.reshape(depth, 1, n))


def _norm_mod_kernel(n_lat, tm, modulated, z_ref, g_ref, *rest):
    if modulated:
        sh_l, sc_l, sh_c, sc_c, o_ref = rest
    else:
        (o_ref,) = rest
    x = z_ref[...]
    r = lax.rsqrt(jnp.mean(x * x, axis=-1, keepdims=True) + EPS)
    h = x * r * g_ref[...]
    if modulated:
        rows = pl.program_id(0) * tm + lax.broadcasted_iota(jnp.int32, (tm, 1), 0)
        is_lat = rows < n_lat
        shift = jnp.where(is_lat, sh_l[0], sh_c[0])
        scale = jnp.where(is_lat, sc_l[0], sc_c[0])
        h = h * (1.0 + scale) + shift
    o_ref[...] = h.astype(o_ref.dtype)


def _norm_mod(z, g, mods, a, b, rows, n_lat, out_dtype):
    d = z.shape[1]
    tm = _row_tile(rows, 256)
    modulated = mods is not None
    in_specs = [pl.BlockSpec((tm, d), lambda i: (i, 0)),
                pl.BlockSpec((1, d), lambda i: (0, 0))]
    args = [z, g.reshape(1, d)]
    if modulated:
        for idx in (a, b, N_MOD + a, N_MOD + b):
            in_specs.append(pl.BlockSpec((1, 1, d), lambda i, idx=idx: (idx, 0, 0)))
            args.append(mods)
    return pl.pallas_call(
        functools.partial(_norm_mod_kernel, n_lat, tm, modulated),
        grid=(rows // tm,),
        in_specs=in_specs,
        out_specs=pl.BlockSpec((tm, d), lambda i: (i, 0)),
        out_shape=jax.ShapeDtypeStruct((rows, d), out_dtype),
        compiler_params=_cparams("parallel"),
        name="norm_mod",
    )(*args)


def _rope_swap(x):
    lane = lax.broadcasted_iota(jnp.int32, x.shape, 1)
    first_half = (lane % (2 * ROPE_PAIRS)) < ROPE_PAIRS
    return jnp.where(first_half, pltpu.roll(x, HEAD_DIM - ROPE_PAIRS, 1), pltpu.roll(x, ROPE_PAIRS, 1))


def _proj_in_kernel(norm, rope, scale, ones_ext, heads, a_ref, w_ref, *rest):
    rest = list(rest)
    g_ref = rest.pop(0) if norm else None
    if rope:
        c_ref = rest.pop(0)
        s_ref = rest.pop(0)
    (o_ref,) = rest
    acc = jnp.dot(a_ref[...], w_ref[...], preferred_element_type=F32)
    if not (norm or rope or ones_ext):
        if scale != 1.0:
            acc = acc * scale
        o_ref[...] = acc.astype(o_ref.dtype)
        return
    ow = EXT if ones_ext else HEAD_DIM
    for hh in range(heads):
        x = acc[:, hh * HEAD_DIM:(hh + 1) * HEAD_DIM]
        if norm:
            r = lax.rsqrt(jnp.mean(x * x, axis=-1, keepdims=True) + EPS)
            x = x * r * g_ref[...]
        if rope:
            x = x * c_ref[...] + _rope_swap(x) * s_ref[...]
        if scale != 1.0:
            x = x * scale
        o_ref[:, hh * ow:hh * ow + HEAD_DIM] = x.astype(o_ref.dtype)
        if ones_ext:
            o_ref[:, hh * ow + HEAD_DIM:(hh + 1) * ow] = jnp.ones(x.shape, o_ref.dtype)


def _proj_in(h, w, col0, width, rows, *, gain=None, rope_tabs=None, scale=1.0, ones_ext=False):
    k = h.shape[1]
    tm = _row_tile(rows, 1280)
    tn = math.gcd(math.gcd(width, 512), col0) if col0 else math.gcd(width, 512)
    norm = gain is not None
    rope = rope_tabs is not None
    in_specs = [pl.BlockSpec((tm, k), lambda j, i: (i, 0)),
                pl.BlockSpec((k, tn), lambda j, i: (0, col0 // tn + j))]
    args = [h, w]
    if norm:
        in_specs.append(pl.BlockSpec((1, HEAD_DIM), lambda j, i: (0, 0)))
        args.append(gain.reshape(1, HEAD_DIM))
    if rope:
        for tab in rope_tabs:
            in_specs.append(pl.BlockSpec((tm, HEAD_DIM), lambda j, i: (i, 0)))
            args.append(tab)
    mult = 2 if ones_ext else 1
    return pl.pallas_call(
        functools.partial(_proj_in_kernel, norm, rope, scale, ones_ext, tn // HEAD_DIM),
        grid=(width // tn, rows // tm),
        in_specs=in_specs,
        out_specs=pl.BlockSpec((tm, mult * tn), lambda j, i: (i, j)),
        out_shape=jax.ShapeDtypeStruct((rows, mult * width), BF16),
        compiler_params=_cparams("parallel", "parallel"),
        name="proj_in",
    )(*args)


def _proj_glu_kernel(a_ref, wa_ref, wg_ref, o_ref):
    a = a_ref[...]
    va = jnp.dot(a, wa_ref[...], preferred_element_type=F32)
    vg = jnp.dot(a, wg_ref[...], preferred_element_type=F32)
    o_ref[...] = (va * _sigmoid(vg)).astype(o_ref.dtype)


def _proj_glu(h, w, col0, width, rows):
    k = h.shape[1]
    tm = _row_tile(rows, 1280)
    tn = math.gcd(math.gcd(width, 512), col0)
    return pl.pallas_call(
        _proj_glu_kernel,
        grid=(width // tn, rows // tm),
        in_specs=[pl.BlockSpec((tm, k), lambda j, i: (i, 0)),
                  pl.BlockSpec((k, tn), lambda j, i: (0, col0 // tn + j)),
                  pl.BlockSpec((k, tn), lambda j, i: (0, (col0 + width) // tn + j))],
        out_specs=pl.BlockSpec((tm, tn), lambda j, i: (i, j)),
        out_shape=jax.ShapeDtypeStruct((rows, width), F32),
        compiler_params=_cparams("parallel", "parallel"),
        name="proj_glu",
    )(h, w, w)


def _proj_out_kernel(n_lat, tm, a1_ref, a2_ref, w1_ref, w2_ref, z_ref, gl_ref, gc_ref, o_ref):
    acc = jnp.dot(a1_ref[...], w1_ref[...], preferred_element_type=F32)
    acc = acc + jnp.dot(a2_ref[...], w2_ref[...], preferred_element_type=F32)
    rows = pl.program_id(1) * tm + lax.broadcasted_iota(jnp.int32, (tm, 1), 0)
    gate = jnp.where(rows < n_lat, gl_ref[0], gc_ref[0])
    o_ref[...] = z_ref[...] + gate * acc


def _proj_out(a1, a2, w, z, mods, gate_idx, rows, n_lat):
    k1, k2 = a1.shape[1], a2.shape[1]
    assert k1 == k2
    n = w.shape[1]
    tm = _row_tile(rows, 1280)
    tn = math.gcd(n, 512)
    return pl.pallas_call(
        functools.partial(_proj_out_kernel, n_lat, tm),
        grid=(n // tn, rows // tm),
        in_specs=[pl.BlockSpec((tm, k1), lambda j, i: (i, 0)),
                  pl.BlockSpec((tm, k2), lambda j, i: (i, 0)),
                  pl.BlockSpec((k1, tn), lambda j, i: (0, j)),
                  pl.BlockSpec((k2, tn), lambda j, i: (1, j)),
                  pl.BlockSpec((tm, tn), lambda j, i: (i, j)),
                  pl.BlockSpec((1, 1, tn), lambda j, i: (gate_idx, 0, j)),
                  pl.BlockSpec((1, 1, tn), lambda j, i: (N_MOD + gate_idx, 0, j))],
        out_specs=pl.BlockSpec((tm, tn), lambda j, i: (i, j)),
        out_shape=jax.ShapeDtypeStruct((rows, n), F32),
        compiler_params=_cparams("parallel", "parallel"),
        name="proj_out",
    )(a1, a2, w, w, z, mods, mods)


def _nt_dot(a, b):
    return lax.dot_general(a, b, (((1,), (1,)), ((), ())), preferred_element_type=F32)


def _softmax_pv(s, v_ext, m_ref, acc_ref):
    m_prev = m_ref[...]
    m_new = jnp.maximum(m_prev, jnp.max(s, axis=1, keepdims=True))
    alpha = jnp.exp2(m_prev - m_new)
    p = jnp.exp2(s - jnp.tile(m_new, (1, s.shape[1] // LANES)))
    pv = jnp.dot(p.astype(v_ext.dtype), v_ext, preferred_element_type=F32)
    acc_ref[...] = jnp.tile(alpha, (1, EXT // LANES)) * acc_ref[...] + pv
    m_ref[...] = m_new


def _softmax_chunk(q, k, v_ext, m_ref, acc_ref, mask=None):
    s = _nt_dot(q, k)
    if mask is not None:
        s = jnp.where(mask, s, NEG)
    _softmax_pv(s, v_ext, m_ref, acc_ref)


def _stack_heads(groups, tq, q_ref, qs_ref):
    for g in range(groups):
        qs_ref[g * tq:(g + 1) * tq, :] = q_ref[:, g * HEAD_DIM:(g + 1) * HEAD_DIM]


def _write_heads(groups, tq, acc_ref, o_ref):
    for g in range(groups):
        a = acc_ref[g * tq:(g + 1) * tq, :]
        o_ref[:, g * HEAD_DIM:(g + 1) * HEAD_DIM] = (a[:, :HEAD_DIM] / a[:, HEAD_DIM:]).astype(o_ref.dtype)


def _global_attn_kernel(groups, tq, kc, n_chunks, q_ref, k_ref, v_ref, kc_ref, vc_ref, o_ref,
                        qs_ref, m_ref, acc_ref, s0_ref, s1_ref):
    _stack_heads(groups, tq, q_ref, qs_ref)
    m_ref[...] = jnp.full(m_ref.shape, NEG, F32)
    acc_ref[...] = jnp.zeros(acc_ref.shape, F32)
    _softmax_chunk(qs_ref[...], kc_ref[...], vc_ref[...], m_ref, acc_ref)

    def logits(c):
        return _nt_dot(qs_ref[...], k_ref[pl.ds(pl.multiple_of(c * kc, kc), kc), :])

    def update(s_ref, c):
        _softmax_pv(s_ref[...], v_ref[pl.ds(pl.multiple_of(c * kc, kc), kc), :], m_ref, acc_ref)

    s0_ref[...] = logits(0)

    def body(t, carry):
        c = 2 * t
        s1_ref[...] = logits(c + 1)
        update(s0_ref, c)
        s0_ref[...] = logits(c + 2)
        update(s1_ref, c + 1)
        return carry

    lax.fori_loop(0, n_chunks // 2 - 1, body, 0)
    s1_ref[...] = logits(n_chunks - 1)
    update(s0_ref, n_chunks - 2)
    update(s1_ref, n_chunks - 1)
    _write_heads(groups, tq, acc_ref, o_ref)


def _global_attn(q, k, v_ext, n_lat, n_ctx, out_rows):
    hq = q.shape[1] // HEAD_DIM
    hkv = k.shape[1] // HEAD_DIM
    groups = hq // hkv
    tq = _row_tile(n_lat, 256)
    kc = _row_tile(n_lat, 512)
    n_chunks = n_lat // kc
    assert n_chunks >= 2 and n_chunks % 2 == 0
    gw = groups * HEAD_DIM
    ctx_blk = n_lat // n_ctx
    rows = groups * tq
    return pl.pallas_call(
        functools.partial(_global_attn_kernel, groups, tq, kc, n_chunks),
        grid=(hkv, n_lat // tq),
        in_specs=[pl.BlockSpec((tq, gw), lambda h, i: (i, h)),
                  pl.BlockSpec((n_lat, HEAD_DIM), lambda h, i: (0, h)),
                  pl.BlockSpec((n_lat, EXT), lambda h, i: (0, h)),
                  pl.BlockSpec((n_ctx, HEAD_DIM), lambda h, i: (ctx_blk, h)),
                  pl.BlockSpec((n_ctx, EXT), lambda h, i: (ctx_blk, h))],
        out_specs=pl.BlockSpec((tq, gw), lambda h, i: (i, h)),
        out_shape=jax.ShapeDtypeStruct((out_rows, q.shape[1]), BF16),
        scratch_shapes=[pltpu.VMEM((rows, HEAD_DIM), BF16),
                        pltpu.VMEM((rows, LANES), F32),
                        pltpu.VMEM((rows, EXT), F32),
                        pltpu.VMEM((rows, kc), F32),
                        pltpu.VMEM((rows, kc), F32)],
        compiler_params=_cparams("parallel", "arbitrary"),
        name="global_attn",
    )(q, k, v_ext, k, v_ext)


def _window_attn_kernel(groups, tq, n_lat, sink_ref, q_ref, k_ref, v_ref, kc_ref, vc_ref, o_ref,
                        qs_ref, m_ref, acc_ref):
    h = pl.program_id(0)
    i = pl.program_id(1)
    s_idx = pl.program_id(2)
    rows = groups * tq

    @pl.when(s_idx == 0)
    def _():
        _stack_heads(groups, tq, q_ref, qs_ref)
        for g in range(groups):
            m_ref[g * tq:(g + 1) * tq, :] = jnp.full((tq, LANES), sink_ref[h * groups + g], F32)
        lane = lax.broadcasted_iota(jnp.int32, (rows, EXT), 1)
        acc_ref[...] = jnp.where(lane >= HEAD_DIM, 1.0, 0.0).astype(F32)
        _softmax_chunk(qs_ref[...], kc_ref[...], vc_ref[...], m_ref, acc_ref)

    r = lax.broadcasted_iota(jnp.int32, (rows, tq), 0)
    qpos = i * tq + (r & (tq - 1))
    kpos = (i + s_idx - 1) * tq + lax.broadcasted_iota(jnp.int32, (rows, tq), 1)
    ok = (jnp.abs(qpos - kpos) <= WINDOW) & (kpos >= 0) & (kpos < n_lat)
    _softmax_chunk(qs_ref[...], k_ref[...], v_ref[...], m_ref, acc_ref, mask=ok)

    @pl.when(s_idx == pl.num_programs(2) - 1)
    def _():
        _write_heads(groups, tq, acc_ref, o_ref)


def _window_attn(q, k, v_ext, sink, n_lat, n_ctx, out_rows):
    hq = q.shape[1] // HEAD_DIM
    hkv = k.shape[1] // HEAD_DIM
    groups = hq // hkv
    tq = _row_tile(n_lat, 256)
    assert tq >= WINDOW and tq & (tq - 1) == 0
    nb = n_lat // tq
    gw = groups * HEAD_DIM
    ctx_blk = n_lat // n_ctx

    def kv_map(h, i, s):
        return (jnp.clip(i + s - 1, 0, nb - 1), h)

    return pl.pallas_call(
        functools.partial(_window_attn_kernel, groups, tq, n_lat),
        grid=(hkv, nb, 3),
        in_specs=[pl.BlockSpec(memory_space=pltpu.SMEM),
                  pl.BlockSpec((tq, gw), lambda h, i, s: (i, h)),
                  pl.BlockSpec((tq, HEAD_DIM), kv_map),
                  pl.BlockSpec((tq, EXT), kv_map),
                  pl.BlockSpec((n_ctx, HEAD_DIM), lambda h, i, s: (ctx_blk, h)),
                  pl.BlockSpec((n_ctx, EXT), lambda h, i, s: (ctx_blk, h))],
        out_specs=pl.BlockSpec((tq, gw), lambda h, i, s: (i, h)),
        out_shape=jax.ShapeDtypeStruct((out_rows, q.shape[1]), BF16),
        scratch_shapes=[pltpu.VMEM((groups * tq, HEAD_DIM), BF16),
                        pltpu.VMEM((groups * tq, LANES), F32),
                        pltpu.VMEM((groups * tq, EXT), F32)],
        compiler_params=_cparams("parallel", "parallel", "arbitrary"),
        name="window_attn",
    )(sink, q, k, v_ext, k, v_ext)


def _ctx_attn_kernel(groups, has_sink, *refs):
    if has_sink:
        sink_ref, q_ref, k_ref, v_ref, o_ref = refs
    else:
        q_ref, k_ref, v_ref, o_ref = refs
    h = pl.program_id(0)
    for g in range(groups):
        q = q_ref[:, g * HEAD_DIM:(g + 1) * HEAD_DIM]
        s = _nt_dot(q, k_ref[...])
        m = jnp.max(s, axis=-1, keepdims=True)
        if has_sink:
            snk = sink_ref[h * groups + g]
            m = jnp.maximum(m, snk)
        p = jnp.exp2(s - m)
        o = jnp.dot(p.astype(v_ref.dtype), v_ref[...], preferred_element_type=F32)
        l = o[:, HEAD_DIM:]
        if has_sink:
            l = l + jnp.exp2(snk - m)
        o_ref[:, g * HEAD_DIM:(g + 1) * HEAD_DIM] = (o[:, :HEAD_DIM] / l).astype(o_ref.dtype)


def _ctx_attn(q, k, v_ext, sink, n_lat, n_ctx):
    hq = q.shape[1] // HEAD_DIM
    hkv = k.shape[1] // HEAD_DIM
    groups = hq // hkv
    gw = groups * HEAD_DIM
    ctx_blk = n_lat // n_ctx
    has_sink = sink is not None
    in_specs = [pl.BlockSpec((n_ctx, gw), lambda h: (ctx_blk, h)),
                pl.BlockSpec((n_ctx, HEAD_DIM), lambda h: (ctx_blk, h)),
                pl.BlockSpec((n_ctx, EXT), lambda h: (ctx_blk, h))]
    args = [q, k, v_ext]
    if has_sink:
        in_specs.insert(0, pl.BlockSpec(memory_space=pltpu.SMEM))
        args.insert(0, sink)
    return pl.pallas_call(
        functools.partial(_ctx_attn_kernel, groups, has_sink),
        grid=(hkv,),
        in_specs=in_specs,
        out_specs=pl.BlockSpec((n_ctx, gw), lambda h: (0, h)),
        out_shape=jax.ShapeDtypeStruct((n_ctx, q.shape[1]), BF16),
        compiler_params=_cparams("parallel"),
        name="ctx_attn",
    )(*args)


def _na_bias_table(rpb, n_rows):
    kh = min(NA_KH, n_rows)
    assert kh == NA_KH and n_rows % NA_ROWS == 0 and n_rows // NA_ROWS >= 2
    qr = np.arange(NA_ROWS)
    krel = np.arange(3 * NA_ROWS) - NA_ROWS
    ri = np.zeros((3, NA_ROWS, 3 * NA_ROWS), np.int32)
    rvalid = np.zeros((3, NA_ROWS, 3 * NA_ROWS), bool)
    big = 4 * NA_ROWS
    for var, (r0, rows) in enumerate(((0, big), (NA_ROWS, big), (n_rows - NA_ROWS, n_rows))):
        r = r0 + qr[:, None]
        key = r0 + krel[None, :]
        rs = np.clip(r - kh // 2, 0, rows - kh)
        rvalid[var] = (key >= rs) & (key < rs + kh)
        ri[var] = np.clip(key - r + NA_KH - 1, 0, 2 * NA_KH - 2)
    qc = np.arange(GRID_W)
    cs = np.clip(qc - NA_KW // 2, 0, GRID_W - NA_KW)
    kc = np.arange(GRID_W)
    cvalid = (kc[None, :] >= cs[:, None]) & (kc[None, :] < cs[:, None] + NA_KW)
    ci = np.clip(kc[None, :] - qc[:, None] + NA_KW - 1, 0, 2 * NA_KW - 2)
    t = jnp.take(rpb * LOG2E, jnp.asarray(ci), axis=2)
    t = jnp.take(t, jnp.asarray(ri), axis=1)
    valid = rvalid[:, :, :, None, None] & cvalid[None, None, None, :, :]
    t = jnp.where(jnp.asarray(valid)[None], t, NEG)
    t = t.transpose(0, 1, 2, 4, 3, 5)
    nq = NA_ROWS * GRID_W
    return t.reshape(rpb.shape[0], 3, nq, 3 * nq)


def _na_kernel(nq, q_ref, kp_ref, k0_ref, kn_ref, vp_ref, v0_ref, vn_ref, kc_ref, vc_ref, b_ref, o_ref):
    q = q_ref[...]
    parts = []
    for idx, k_ref in enumerate((kp_ref, k0_ref, kn_ref)):
        parts.append(_nt_dot(q, k_ref[...]) + b_ref[0, 0, :, idx * nq:(idx + 1) * nq])
    parts.append(_nt_dot(q, kc_ref[...]))
    m = parts[0].max(axis=-1, keepdims=True)
    for s in parts[1:]:
        m = jnp.maximum(m, s.max(axis=-1, keepdims=True))
    o = jnp.zeros((nq, EXT), F32)
    for s, v_ref in zip(parts, (vp_ref, v0_ref, vn_ref, vc_ref)):
        p = jnp.exp2(s - m)
        o = o + jnp.dot(p.astype(v_ref.dtype), v_ref[...], preferred_element_type=F32)
    o_ref[...] = (o[:, :HEAD_DIM] / o[:, HEAD_DIM:]).astype(o_ref.dtype)


def _na_attn(q, k, v_ext, bias, n_lat, n_ctx, out_rows):
    heads = q.shape[1] // HEAD_DIM
    nq = NA_ROWS * GRID_W
    nb = n_lat // nq
    ctx_blk = n_lat // n_ctx

    def prev_map(h, b):
        return (jnp.maximum(b - 1, 0), h)

    def next_map(h, b):
        return (jnp.minimum(b + 1, nb - 1), h)

    def own_map(h, b):
        return (b, h)

    def ctx_map(h, b):
        return (ctx_blk, h)

    def bias_map(h, b):
        return (h, jnp.where(b == 0, 0, jnp.where(b == nb - 1, 2, 1)), 0, 0)

    blk = (nq, HEAD_DIM)
    vblk = (nq, EXT)
    return pl.pallas_call(
        functools.partial(_na_kernel, nq),
        grid=(heads, nb),
        in_specs=[pl.BlockSpec(blk, own_map),
                  pl.BlockSpec(blk, prev_map), pl.BlockSpec(blk, own_map), pl.BlockSpec(blk, next_map),
                  pl.BlockSpec(vblk, prev_map), pl.BlockSpec(vblk, own_map), pl.BlockSpec(vblk, next_map),
                  pl.BlockSpec((n_ctx, HEAD_DIM), ctx_map), pl.BlockSpec((n_ctx, EXT), ctx_map),
                  pl.BlockSpec((1, 1, nq, 3 * nq), bias_map)],
        out_specs=pl.BlockSpec(blk, own_map),
        out_shape=jax.ShapeDtypeStruct((out_rows, q.shape[1]), BF16),
        compiler_params=_cparams("parallel", "arbitrary"),
        name="na_attn",
    )(q, k, k, k, v_ext, v_ext, v_ext, k, v_ext, bias)


def _conv_kernel(tr, n_lat_tiles, n_tiles, ch_chunk, prev_ref, cur_ref, next_ref, w_ref, b_ref,
                 g_ref, beta_ref, o_ref, win_ref, y_ref):
    i = pl.program_id(0)
    c = cur_ref.shape[1]
    prev_ok = jnp.logical_and(i != 0, i != n_lat_tiles)
    next_ok = jnp.logical_and(i != n_lat_tiles - 1, i != n_tiles - 1)
    win_ref[0:CONV_HALO, :] = jnp.where(prev_ok, prev_ref[...], 0.0)
    win_ref[CONV_HALO:CONV_HALO + tr, :] = cur_ref[...]
    win_ref[CONV_HALO + tr:, :] = jnp.where(next_ok, next_ref[...], 0.0)
    off = CONV_HALO - CONV_W // 2
    rc = 64

    def col_body(cc, carry):
        c0 = pl.multiple_of(cc * ch_chunk, ch_chunk)
        for r0 in range(0, tr, rc):
            acc = jnp.zeros((rc, ch_chunk), F32)
            for t in range(CONV_W):
                acc = acc + win_ref[r0 + off + t:r0 + off + t + rc, pl.ds(c0, ch_chunk)] * \
                    w_ref[t:t + 1, pl.ds(c0, ch_chunk)]
            y_ref[r0:r0 + rc, pl.ds(c0, ch_chunk)] = acc + b_ref[:, pl.ds(c0, ch_chunk)]
        return carry

    lax.fori_loop(0, c // ch_chunk, col_body, 0)
    y = y_ref[...]
    mu = jnp.mean(y, axis=-1, keepdims=True)
    var = jnp.mean(jnp.square(y - mu), axis=-1, keepdims=True)
    yn = (y - mu) * lax.rsqrt(var + EPS) * g_ref[...] + beta_ref[...]
    o_ref[...] = (yn * _sigmoid(yn)).astype(o_ref.dtype)


def _conformer_conv(u, dw_w, dw_b, ln_g, ln_b, rows, n_lat):
    c = u.shape[1]
    tr = 256
    assert n_lat % tr == 0 and rows % tr == 0
    hb = tr // CONV_HALO
    n_tiles = rows // tr
    last_halo = rows // CONV_HALO - 1
    ch_chunk = math.gcd(c, 256)
    return pl.pallas_call(
        functools.partial(_conv_kernel, tr, n_lat // tr, n_tiles, ch_chunk),
        grid=(n_tiles,),
        in_specs=[pl.BlockSpec((CONV_HALO, c), lambda i: (jnp.maximum(i * hb - 1, 0), 0)),
                  pl.BlockSpec((tr, c), lambda i: (i, 0)),
                  pl.BlockSpec((CONV_HALO, c), lambda i: (jnp.minimum((i + 1) * hb, last_halo), 0)),
                  pl.BlockSpec((CONV_W, c), lambda i: (0, 0)),
                  pl.BlockSpec((1, c), lambda i: (0, 0)),
                  pl.BlockSpec((1, c), lambda i: (0, 0)),
                  pl.BlockSpec((1, c), lambda i: (0, 0))],
        out_specs=pl.BlockSpec((tr, c), lambda i: (i, 0)),
        out_shape=jax.ShapeDtypeStruct((rows, c), BF16),
        scratch_shapes=[pltpu.VMEM((tr + 2 * CONV_HALO, c), F32),
                        pltpu.VMEM((tr, c), F32)],
        compiler_params=_cparams("parallel"),
        name="conformer_conv",
    )(u, u, u, dw_w, dw_b.reshape(1, c), ln_g.reshape(1, c), ln_b.reshape(1, c))


R_EID, R_WT, R_RANK = 0, 2, 4


def _router_kernel(n_groups, per_group, tm, h_ref, w_ref, info_ref, cnt_ref, carry_ref):
    @pl.when(pl.program_id(0) == 0)
    def _():
        carry_ref[...] = jnp.zeros(carry_ref.shape, F32)

    lg = jnp.dot(h_ref[...], w_ref[...], preferred_element_type=F32)
    lane = lax.broadcasted_iota(jnp.int32, lg.shape, 1).astype(F32)
    is_g = lane < n_groups
    gl = jnp.where(is_g, lg, NEG)
    mg = jnp.max(gl, axis=-1, keepdims=True)
    gsel = jnp.min(jnp.where(gl == mg, lane, float(LANES)), axis=-1, keepdims=True)
    gsum = jnp.sum(jnp.where(is_g, jnp.exp(gl - mg), 0.0), axis=-1, keepdims=True)
    lo = n_groups + gsel * per_group
    el = jnp.where((lane >= lo) & (lane < lo + per_group), lg, NEG)
    v1 = jnp.max(el, axis=-1, keepdims=True)
    i1 = jnp.min(jnp.where(el == v1, lane, float(LANES)), axis=-1, keepdims=True)
    el2 = jnp.where(lane == i1, NEG, el)
    v2 = jnp.max(el2, axis=-1, keepdims=True)
    i2 = jnp.min(jnp.where(el2 == v2, lane, float(LANES)), axis=-1, keepdims=True)
    e2 = jnp.exp(v2 - v1)
    w1 = 1.0 / ((1.0 + e2) * gsum)
    w2 = e2 * w1
    hit1 = lane == i1
    hit2 = lane == i2
    cnt = jnp.where(hit1 | hit2, 1.0, 0.0)
    tri = (lax.broadcasted_iota(jnp.int32, (tm, tm), 1) < lax.broadcasted_iota(jnp.int32, (tm, tm), 0))
    before = jnp.dot(jnp.where(tri, 1.0, 0.0).astype(BF16), cnt.astype(BF16), preferred_element_type=F32)
    tot = carry_ref[0:1, :] + before
    r1 = jnp.sum(jnp.where(hit1, tot, 0.0), axis=-1, keepdims=True)
    r2 = jnp.sum(jnp.where(hit2, tot, 0.0), axis=-1, keepdims=True)
    carry_ref[...] = carry_ref[...] + jnp.sum(cnt, axis=0, keepdims=True)
    info = jnp.zeros(lg.shape, F32)
    for col, val in ((R_EID, i1 - n_groups), (R_EID + 1, i2 - n_groups), (R_WT, w1), (R_WT + 1, w2),
                     (R_RANK, r1), (R_RANK + 1, r2)):
        info = jnp.where(lane == col, val, info)
    info_ref[...] = info
    cnt_ref[...] = carry_ref[...]


def _router(h, w_r, n_groups, per_group, rows):
    d = h.shape[1]
    tm = _row_tile(rows, 512)
    return pl.pallas_call(
        functools.partial(_router_kernel, n_groups, per_group, tm),
        grid=(rows // tm,),
        in_specs=[pl.BlockSpec((tm, d), lambda i: (i, 0)),
                  pl.BlockSpec((d, LANES), lambda i: (0, 0))],
        out_specs=[pl.BlockSpec((tm, LANES), lambda i: (i, 0)),
                   pl.BlockSpec((8, LANES), lambda i: (0, 0))],
        out_shape=[jax.ShapeDtypeStruct((rows, LANES), F32),
                   jax.ShapeDtypeStruct((8, LANES), F32)],
        scratch_shapes=[pltpu.VMEM((8, LANES), F32)],
        compiler_params=_cparams("arbitrary"),
        name="moe_router",
    )(h, w_r)


def _expert_kernel(te_ref, tv_ref, x_ref, wg_ref, wu_ref, wd_ref, o_ref):
    i = pl.program_id(0)

    @pl.when(tv_ref[i] > 0)
    def _():
        x = x_ref[...]
        hg = jnp.dot(x, wg_ref[0], preferred_element_type=F32)
        hu = jnp.dot(x, wu_ref[0], preferred_element_type=F32)
        act = (hg * _sigmoid(hg)) * hu
        o_ref[...] = jnp.dot(act.astype(wd_ref.dtype), wd_ref[0],
                             preferred_element_type=F32).astype(o_ref.dtype)

    @pl.when(tv_ref[i] == 0)
    def _():
        o_ref[...] = jnp.zeros(o_ref.shape, o_ref.dtype)


def _expert_ffn(xs, tile_e, tile_valid, w_g, w_u, w_d, tm):
    p, d = xs.shape
    f = w_g.shape[2]
    grid_spec = pltpu.PrefetchScalarGridSpec(
        num_scalar_prefetch=2,
        grid=(p // tm,),
        in_specs=[pl.BlockSpec((tm, d), lambda i, te, tv: (i, 0)),
                  pl.BlockSpec((1, d, f), lambda i, te, tv: (te[i], 0, 0)),
                  pl.BlockSpec((1, d, f), lambda i, te, tv: (te[i], 0, 0)),
                  pl.BlockSpec((1, f, d), lambda i, te, tv: (te[i], 0, 0))],
        out_specs=pl.BlockSpec((tm, d), lambda i, te, tv: (i, 0)),
    )
    return pl.pallas_call(
        _expert_kernel,
        grid_spec=grid_spec,
        out_shape=jax.ShapeDtypeStruct((p, d), BF16),
        compiler_params=_cparams("arbitrary"),
        name="moe_experts",
    )(tile_e, tile_valid, xs, w_g, w_u, w_d)


def _combine_kernel(n_lat, tm, d, z_ref, y_ref, info_ref, gl_ref, gc_ref, o_ref):
    rows = pl.program_id(0) * tm + lax.broadcasted_iota(jnp.int32, (tm, 1), 0)
    gate = jnp.where(rows < n_lat, gl_ref[0], gc_ref[0])
    w1 = info_ref[:, R_WT:R_WT + 1]
    w2 = info_ref[:, R_WT + 1:R_WT + 2]
    y = w1 * y_ref[:, :d].astype(F32) + w2 * y_ref[:, d:].astype(F32)
    o_ref[...] = z_ref[...] + gate * y


def _combine(z, y_pairs, info, mods, gate_idx, rows, n_lat):
    d = z.shape[1]
    tm = _row_tile(rows, 256)
    return pl.pallas_call(
        functools.partial(_combine_kernel, n_lat, tm, d),
        grid=(rows // tm,),
        in_specs=[pl.BlockSpec((tm, d), lambda i: (i, 0)),
                  pl.BlockSpec((tm, 2 * d), lambda i: (i, 0)),
                  pl.BlockSpec((tm, LANES), lambda i: (i, 0)),
                  pl.BlockSpec((1, 1, d), lambda i: (gate_idx, 0, 0)),
                  pl.BlockSpec((1, 1, d), lambda i: (N_MOD + gate_idx, 0, 0))],
        out_specs=pl.BlockSpec((tm, d), lambda i: (i, 0)),
        out_shape=jax.ShapeDtypeStruct((rows, d), F32),
        compiler_params=_cparams("parallel"),
        name="moe_combine",
    )(z, y_pairs, info, mods, mods)


def _rows_of(x, idx):
    return x.at[idx].get(mode="promise_in_bounds", unique_indices=False)


def _moe(h2, z, mods, w_r, w_g, w_u, w_d, n_groups, per_group, rows, n_lat):
    n_exp = n_groups * per_group
    tm = MOE_TILE
    d = h2.shape[1]
    info, cnt = _router(h2, w_r, n_groups, per_group, rows)
    eid = info[:, R_EID:R_EID + EXPERT_TOPK].astype(jnp.int32)
    rank = info[:, R_RANK:R_RANK + EXPERT_TOPK].astype(jnp.int32)
    counts = cnt[0, n_groups:n_groups + n_exp].astype(jnp.int32)
    padded = ((counts + tm - 1) // tm) * tm
    ends = jnp.cumsum(padded)
    starts = ends - padded
    pos = (starts[eid] + rank).reshape(-1)
    n_pairs = rows * EXPERT_TOPK
    n_tiles = (n_pairs + n_exp * (tm - 1)) // tm + 1
    p = n_tiles * tm
    pair_token = jnp.arange(n_pairs, dtype=jnp.int32) // EXPERT_TOPK
    row_token = jnp.zeros((p,), jnp.int32).at[pos].set(pair_token, mode="promise_in_bounds",
                                                      unique_indices=True)
    tile_start = jnp.arange(n_tiles, dtype=jnp.int32) * tm
    tile_e = jnp.minimum(jnp.searchsorted(ends, tile_start, side="right"), n_exp - 1).astype(jnp.int32)
    tile_valid = (tile_start < ends[-1]).astype(jnp.int32)
    xs = _rows_of(h2, row_token)
    ys = _expert_ffn(xs, tile_e, tile_valid, w_g, w_u, w_d, tm)
    y_pairs = _rows_of(ys, pos).reshape(rows, EXPERT_TOPK * d)
    return _combine(z, y_pairs, info, mods, 5, rows, n_lat)


def _rope_tables(n_lat, n_ctx):
    t = jnp.arange(n_lat)
    row = (t // GRID_W).astype(F32)
    col = (t % GRID_W).astype(F32)
    inv = ROPE_THETA ** (-jnp.arange(ROPE_PAIRS, dtype=F32) / ROPE_PAIRS)
    ar = row[:, None] * inv
    ac = col[:, None] * inv
    cos = jnp.concatenate([jnp.cos(ar), jnp.cos(ar), jnp.cos(ac), jnp.cos(ac)], axis=1)
    sin = jnp.concatenate([-jnp.sin(ar), jnp.sin(ar), -jnp.sin(ac), jnp.sin(ac)], axis=1)
    cos = jnp.concatenate([cos, jnp.ones((n_ctx, HEAD_DIM), F32)], axis=0)
    sin = jnp.concatenate([sin, jnp.zeros((n_ctx, HEAD_DIM), F32)], axis=0)
    return cos, sin


def _with_ctx(o_lat, o_ctx, n_lat):
    return lax.dynamic_update_slice(o_lat, o_ctx, (n_lat, 0))


def kernel(x, c, ctx, c_ctx, ada_down, ada_up, ada_bias, norm1_g, norm2_g, w_in_ab, w_out_ab, qnorm_a,
           knorm_a, sink_b, w_in_cd, w_out_cd, rpb_c, dw_w, dw_b, ln_d_g, ln_d_b, router_group,
           router_expert, w_gate, w_up, w_down, final_g):
    assert x.shape[0] == 1 and ctx.shape[0] == 1
    n_lat, d = x.shape[1], x.shape[2]
    n_ctx = ctx.shape[1]
    n_all = n_lat + n_ctx
    depth = ada_down.shape[0]
    assert n_lat % n_ctx == 0 and n_ctx % 128 == 0
    n_groups, per_group = router_expert.shape[1], router_expert.shape[3]
    n_mix = d // HEAD_DIM
    a_w = (n_mix // 2) * HEAD_DIM
    akv_w = (n_mix // 8) * HEAD_DIM
    c_w = (n_mix // 2) * HEAD_DIM
    d_ch = d // 2
    qscale = HEAD_DIM ** -0.5 * LOG2E

    z = jnp.concatenate([x[0], ctx[0]], axis=0)
    cond = jnp.zeros((8, d), F32).at[0].set(c[0]).at[1].set(c_ctx)
    mods_all = _modulation_all(cond, ada_down, ada_up, ada_bias)
    tabs = _rope_tables(n_lat, n_ctx)

    for layer in range(depth):
        need_ctx = layer < depth - 1
        rows = n_all if need_ctx else n_lat
        mods = mods_all[layer, :2].reshape(2 * N_MOD, 1, d)
        h = _norm_mod(z, norm1_g[layer], mods, 0, 1, n_all, n_lat, BF16)
        i = layer // 2
        if layer % 2 == 0:
            w = w_in_ab[i].astype(BF16)
            sink = sink_b[i] * LOG2E
            col = 0
            qa = _proj_in(h, w, col, a_w, n_all, gain=qnorm_a[i], rope_tabs=tabs, scale=qscale)
            col += a_w
            ka = _proj_in(h, w, col, akv_w, n_all, gain=knorm_a[i], rope_tabs=tabs)
            col += akv_w
            va = _proj_in(h, w, col, akv_w, n_all, ones_ext=True)
            col += akv_w
            qb = _proj_in(h, w, col, a_w, n_all, rope_tabs=tabs, scale=qscale)
            col += a_w
            kb = _proj_in(h, w, col, akv_w, n_all, rope_tabs=tabs)
            col += akv_w
            vb = _proj_in(h, w, col, akv_w, n_all, ones_ext=True)
            o1 = _global_attn(qa, ka, va, n_lat, n_ctx, rows)
            o2 = _window_attn(qb, kb, vb, sink, n_lat, n_ctx, rows)
            if need_ctx:
                o1 = _with_ctx(o1, _ctx_attn(qa, ka, va, None, n_lat, n_ctx), n_lat)
                o2 = _with_ctx(o2, _ctx_attn(qb, kb, vb, sink, n_lat, n_ctx), n_lat)
            w_out = w_out_ab[i].astype(BF16)
        else:
            w = w_in_cd[i].astype(BF16)
            qc = _proj_in(h, w, 0, c_w, n_all, scale=qscale)
            kc = _proj_in(h, w, c_w, c_w, n_all)
            vc = _proj_in(h, w, 2 * c_w, c_w, n_all, ones_ext=True)
            u = _proj_glu(h, w, 3 * c_w, d_ch, n_all)
            bias = _na_bias_table(rpb_c[i], n_lat // GRID_W)
            o1 = _na_attn(qc, kc, vc, bias, n_lat, n_ctx, rows)
            if need_ctx:
                o1 = _with_ctx(o1, _ctx_attn(qc, kc, vc, None, n_lat, n_ctx), n_lat)
            o2 = _conformer_conv(u, dw_w[i], dw_b[i], ln_d_g[i], ln_d_b[i], rows, n_lat)
            w_out = w_out_cd[i].astype(BF16)
        z = _proj_out(o1, o2, w_out, z, mods, 2, rows, n_lat)
        h2 = _norm_mod(z, norm2_g[layer], mods, 3, 4, rows, n_lat, BF16)
        w_r = jnp.concatenate(
            [router_group[layer], router_expert[layer].transpose(1, 0, 2).reshape(d, n_groups * per_group)],
            axis=1)
        w_r = jnp.pad(w_r, ((0, 0), (0, LANES - w_r.shape[1]))).astype(BF16)
        z = _moe(h2, z, mods, w_r, w_gate[layer].astype(BF16), w_up[layer].astype(BF16),
                 w_down[layer].astype(BF16), n_groups, per_group, rows, n_lat)
    out = _norm_mod(z, final_g, None, 0, 0, n_lat, n_lat, F32)
    return out[None]
```

```python
import functools
import math

import numpy as np
import jax
import jax.numpy as jnp
from jax import lax
from jax.experimental import pallas as pl
from jax.experimental.pallas import tpu as pltpu

HEAD_DIM = 128
GRID_W = 64
ROPE_PAIRS = HEAD_DIM // 4
ROPE_THETA = 10000.0
WINDOW = 128
NA_KH = 8
NA_KW = 16
CONV_W = 31
N_MOD = 6
EXPERT_TOPK = 2
EPS = 1e-6
NEG = -1e30
LOG2E = math.log2(math.e)
LANES = 128
VMEM_LIMIT = 56 * 1024 * 1024
NA_ROWS = 8
CONV_HALO = 16
EXT = 2 * HEAD_DIM
MOE_TILE = 256

BF16 = jnp.bfloat16
F32 = jnp.float32


def _cparams(*sem):
    return pltpu.CompilerParams(dimension_semantics=sem, vmem_limit_bytes=VMEM_LIMIT)


def _row_tile(m, cap):
    best = None
    for t in range(LANES, min(m, cap) + 1, LANES):
        if m % t == 0:
            best = t
    assert best is not None, (m, cap)
    return best


def _sigmoid(x):
    return 1.0 / (1.0 + jnp.exp(-x))


def _mod_down_kernel(cond_ref, down_ref, o_ref):
    cnd = cond_ref[...]
    a = cnd * _sigmoid(cnd)
    o_ref[0] = jnp.dot(a, down_ref[0], preferred_element_type=F32, precision=lax.Precision.HIGHEST)


def _mod_up_kernel(t_ref, up_ref, b_ref, o_ref):
    o_ref[0] = jnp.dot(t_ref[0], up_ref[0], preferred_element_type=F32,
                       precision=lax.Precision.HIGHEST) + b_ref[0]


def _modulation_all(cond, ada_down, ada_up, ada_bias):
    depth, d, r = ada_down.shape
    n = ada_up.shape[2]
    t = pl.pallas_call(
        _mod_down_kernel,
        grid=(depth,),
        in_specs=[pl.BlockSpec((8, d), lambda l: (0, 0)),
                  pl.BlockSpec((1, d, r), lambda l: (l, 0, 0))],
        out_specs=pl.BlockSpec((1, 8, r), lambda l: (l, 0, 0)),
        out_shape=jax.ShapeDtypeStruct((depth, 8, r), F32),
        compiler_params=_cparams("arbitrary"),
        name="mod_down",
    )(cond, ada_down)
    tn = _row_tile(n, 2048)
    return pl.pallas_call(
        _mod_up_kernel,
        grid=(depth, n // tn),
        in_specs=[pl.BlockSpec((1, 8, r), lambda l, j: (l, 0, 0)),
                  pl.BlockSpec((1, r, tn), lambda l, j: (l, 0, j)),
                  pl.BlockSpec((1, 1, tn), lambda l, j: (l, 0, j))],
        out_specs=pl.BlockSpec((1, 8, tn), lambda l, j: (l, 0, j)),
        out_shape=jax.ShapeDtypeStruct((depth, 8, n), F32),
        compiler_params=_cparams("arbitrary", "arbitrary"),
        name="mod_up",
    )(t, ada_up, ada_bias.reshape(depth, 1, n))


def _norm_mod_kernel(n_lat, tm, modulated, z_ref, g_ref, *rest):
    if modulated:
        sh_l, sc_l, sh_c, sc_c, o_ref = rest
    else:
        (o_ref,) = rest
    x = z_ref[...]
    r = lax.rsqrt(jnp.mean(x * x, axis=-1, keepdims=True) + EPS)
    h = x * r * g_ref[...]
    if modulated:
        rows = pl.program_id(0) * tm + lax.broadcasted_iota(jnp.int32, (tm, 1), 0)
        is_lat = rows < n_lat
        shift = jnp.where(is_lat, sh_l[0], sh_c[0])
        scale = jnp.where(is_lat, sc_l[0], sc_c[0])
        h = h * (1.0 + scale) + shift
    o_ref[...] = h.astype(o_ref.dtype)


def _norm_mod(z, g, mods, a, b, rows, n_lat, out_dtype):
    d = z.shape[1]
    tm = _row_tile(rows, 256)
    modulated = mods is not None
    in_specs = [pl.BlockSpec((tm, d), lambda i: (i, 0)),
                pl.BlockSpec((1, d), lambda i: (0, 0))]
    args = [z, g.reshape(1, d)]
    if modulated:
        for idx in (a, b, N_MOD + a, N_MOD + b):
            in_specs.append(pl.BlockSpec((1, 1, d), lambda i, idx=idx: (idx, 0, 0)))
            args.append(mods)
    return pl.pallas_call(
        functools.partial(_norm_mod_kernel, n_lat, tm, modulated),
        grid=(rows // tm,),
        in_specs=in_specs,
        out_specs=pl.BlockSpec((tm, d), lambda i: (i, 0)),
        out_shape=jax.ShapeDtypeStruct((rows, d), out_dtype),
        compiler_params=_cparams("parallel"),
        name="norm_mod",
    )(*args)


def _rope_swap(x):
    lane = lax.broadcasted_iota(jnp.int32, x.shape, 1)
    first_half = (lane % (2 * ROPE_PAIRS)) < ROPE_PAIRS
    return jnp.where(first_half, pltpu.roll(x, HEAD_DIM - ROPE_PAIRS, 1), pltpu.roll(x, ROPE_PAIRS, 1))


def _cast_weight_tile(w_ref, wb_ref):
    @pl.when(pl.program_id(1) == 0)
    def _():
        wb_ref[...] = w_ref[0].astype(wb_ref.dtype)


def _proj_in_kernel(norm, rope, scale, ones_ext, heads, a_ref, w_ref, *rest):
    rest = list(rest)
    g_ref = rest.pop(0) if norm else None
    if rope:
        c_ref = rest.pop(0)
        s_ref = rest.pop(0)
    o_ref, wb_ref = rest
    _cast_weight_tile(w_ref, wb_ref)
    acc = jnp.dot(a_ref[...], wb_ref[...], preferred_element_type=F32)
    if not (norm or rope or ones_ext):
        if scale != 1.0:
            acc = acc * scale
        o_ref[...] = acc.astype(o_ref.dtype)
        return
    ow = EXT if ones_ext else HEAD_DIM
    for hh in range(heads):
        x = acc[:, hh * HEAD_DIM:(hh + 1) * HEAD_DIM]
        if norm:
            r = lax.rsqrt(jnp.mean(x * x, axis=-1, keepdims=True) + EPS)
            x = x * r * g_ref[...]
        if rope:
            x = x * c_ref[...] + _rope_swap(x) * s_ref[...]
        if scale != 1.0:
            x = x * scale
        o_ref[:, hh * ow:hh * ow + HEAD_DIM] = x.astype(o_ref.dtype)
        if ones_ext:
            o_ref[:, hh * ow + HEAD_DIM:(hh + 1) * ow] = jnp.ones(x.shape, o_ref.dtype)


def _proj_in(h, w, li, col0, width, rows, *, gain=None, rope_tabs=None, scale=1.0, ones_ext=False):
    k = h.shape[1]
    tm = _row_tile(rows, 1280)
    tn = math.gcd(math.gcd(width, 512), col0) if col0 else math.gcd(width, 512)
    norm = gain is not None
    rope = rope_tabs is not None
    in_specs = [pl.BlockSpec((tm, k), lambda j, i: (i, 0)),
                pl.BlockSpec((1, k, tn), lambda j, i: (li, 0, col0 // tn + j))]
    args = [h, w]
    if norm:
        in_specs.append(pl.BlockSpec((1, HEAD_DIM), lambda j, i: (0, 0)))
        args.append(gain.reshape(1, HEAD_DIM))
    if rope:
        for tab in rope_tabs:
            in_specs.append(pl.BlockSpec((tm, HEAD_DIM), lambda j, i: (i, 0)))
            args.append(tab)
    mult = 2 if ones_ext else 1
    return pl.pallas_call(
        functools.partial(_proj_in_kernel, norm, rope, scale, ones_ext, tn // HEAD_DIM),
        grid=(width // tn, rows // tm),
        in_specs=in_specs,
        out_specs=pl.BlockSpec((tm, mult * tn), lambda j, i: (i, j)),
        out_shape=jax.ShapeDtypeStruct((rows, mult * width), BF16),
        scratch_shapes=[pltpu.VMEM((k, tn), BF16)],
        compiler_params=_cparams("parallel", "arbitrary"),
        name="proj_in",
    )(*args)


def _proj_glu_kernel(a_ref, wa_ref, wg_ref, o_ref, wab_ref, wgb_ref):
    _cast_weight_tile(wa_ref, wab_ref)
    _cast_weight_tile(wg_ref, wgb_ref)
    a = a_ref[...]
    va = jnp.dot(a, wab_ref[...], preferred_element_type=F32)
    vg = jnp.dot(a, wgb_ref[...], preferred_element_type=F32)
    o_ref[...] = (va * _sigmoid(vg)).astype(o_ref.dtype)


def _proj_glu(h, w, li, col0, width, rows):
    k = h.shape[1]
    tm = _row_tile(rows, 1280)
    tn = math.gcd(math.gcd(width, 256), col0)
    return pl.pallas_call(
        _proj_glu_kernel,
        grid=(width // tn, rows // tm),
        in_specs=[pl.BlockSpec((tm, k), lambda j, i: (i, 0)),
                  pl.BlockSpec((1, k, tn), lambda j, i: (li, 0, col0 // tn + j)),
                  pl.BlockSpec((1, k, tn), lambda j, i: (li, 0, (col0 + width) // tn + j))],
        out_specs=pl.BlockSpec((tm, tn), lambda j, i: (i, j)),
        out_shape=jax.ShapeDtypeStruct((rows, width), F32),
        scratch_shapes=[pltpu.VMEM((k, tn), BF16), pltpu.VMEM((k, tn), BF16)],
        compiler_params=_cparams("parallel", "arbitrary"),
        name="proj_glu",
    )(h, w, w)


def _proj_out_kernel(n_lat, tm, a1_ref, a2_ref, w1_ref, w2_ref, z_ref, gl_ref, gc_ref, o_ref,
                     w1b_ref, w2b_ref):
    _cast_weight_tile(w1_ref, w1b_ref)
    _cast_weight_tile(w2_ref, w2b_ref)
    acc = jnp.dot(a1_ref[...], w1b_ref[...], preferred_element_type=F32)
    acc = acc + jnp.dot(a2_ref[...], w2b_ref[...], preferred_element_type=F32)
    rows = pl.program_id(1) * tm + lax.broadcasted_iota(jnp.int32, (tm, 1), 0)
    gate = jnp.where(rows < n_lat, gl_ref[0], gc_ref[0])
    o_ref[...] = z_ref[...] + gate * acc


def _proj_out(a1, a2, w, li, z, mods, gate_idx, rows, n_lat):
    k1, k2 = a1.shape[1], a2.shape[1]
    assert k1 == k2
    n = w.shape[2]
    tm = _row_tile(rows, 1280)
    tn = math.gcd(n, 512)
    return pl.pallas_call(
        functools.partial(_proj_out_kernel, n_lat, tm),
        grid=(n // tn, rows // tm),
        in_specs=[pl.BlockSpec((tm, k1), lambda j, i: (i, 0)),
                  pl.BlockSpec((tm, k2), lambda j, i: (i, 0)),
                  pl.BlockSpec((1, k1, tn), lambda j, i: (li, 0, j)),
                  pl.BlockSpec((1, k2, tn), lambda j, i: (li, 1, j)),
                  pl.BlockSpec((tm, tn), lambda j, i: (i, j)),
                  pl.BlockSpec((1, 1, tn), lambda j, i: (gate_idx, 0, j)),
                  pl.BlockSpec((1, 1, tn), lambda j, i: (N_MOD + gate_idx, 0, j))],
        out_specs=pl.BlockSpec((tm, tn), lambda j, i: (i, j)),
        out_shape=jax.ShapeDtypeStruct((rows, n), F32),
        scratch_shapes=[pltpu.VMEM((k1, tn), BF16), pltpu.VMEM((k2, tn), BF16)],
        compiler_params=_cparams("parallel", "arbitrary"),
        name="proj_out",
    )(a1, a2, w, w, z, mods, mods)


def _nt_dot(a, b):
    return lax.dot_general(a, b, (((1,), (1,)), ((), ())), preferred_element_type=F32)


def _softmax_pv(s, v_ext, m_ref, acc_ref):
    m_prev = m_ref[...]
    m_new = jnp.maximum(m_prev, jnp.max(s, axis=1, keepdims=True))
    alpha = jnp.exp2(m_prev - m_new)
    p = jnp.exp2(s - jnp.tile(m_new, (1, s.shape[1] // LANES)))
    pv = jnp.dot(p.astype(v_ext.dtype), v_ext, preferred_element_type=F32)
    acc_ref[...] = jnp.tile(alpha, (1, EXT // LANES)) * acc_ref[...] + pv
    m_ref[...] = m_new


def _softmax_chunk(q, k, v_ext, m_ref, acc_ref, mask=None):
    s = _nt_dot(q, k)
    if mask is not None:
        s = jnp.where(mask, s, NEG)
    _softmax_pv(s, v_ext, m_ref, acc_ref)


def _stack_heads(groups, tq, q_ref, qs_ref):
    for g in range(groups):
        qs_ref[g * tq:(g + 1) * tq, :] = q_ref[:, g * HEAD_DIM:(g + 1) * HEAD_DIM]


def _write_heads(groups, tq, acc_ref, o_ref):
    for g in range(groups):
        a = acc_ref[g * tq:(g + 1) * tq, :]
        o_ref[:, g * HEAD_DIM:(g + 1) * HEAD_DIM] = (a[:, :HEAD_DIM] / a[:, HEAD_DIM:]).astype(o_ref.dtype)


def _global_attn_kernel(groups, tq, kc, n_chunks, q_ref, k_ref, v_ref, kc_ref, vc_ref, o_ref,
                        qs_ref, m_ref, acc_ref, s0_ref, s1_ref):
    _stack_heads(groups, tq, q_ref, qs_ref)
    m_ref[...] = jnp.full(m_ref.shape, NEG, F32)
    acc_ref[...] = jnp.zeros(acc_ref.shape, F32)
    _softmax_chunk(qs_ref[...], kc_ref[...], vc_ref[...], m_ref, acc_ref)

    def logits(c):
        return _nt_dot(qs_ref[...], k_ref[pl.ds(pl.multiple_of(c * kc, kc), kc), :])

    def update(s_ref, c):
        _softmax_pv(s_ref[...], v_ref[pl.ds(pl.multiple_of(c * kc, kc), kc), :], m_ref, acc_ref)

    s0_ref[...] = logits(0)

    def body(t, carry):
        c = 2 * t
        s1_ref[...] = logits(c + 1)
        update(s0_ref, c)
        s0_ref[...] = logits(c + 2)
        update(s1_ref, c + 1)
        return carry

    lax.fori_loop(0, n_chunks // 2 - 1, body, 0)
    s1_ref[...] = logits(n_chunks - 1)
    update(s0_ref, n_chunks - 2)
    update(s1_ref, n_chunks - 1)
    _write_heads(groups, tq, acc_ref, o_ref)


def _global_attn(q, k, v_ext, n_lat, n_ctx, out_rows):
    hq = q.shape[1] // HEAD_DIM
    hkv = k.shape[1] // HEAD_DIM
    groups = hq // hkv
    tq = _row_tile(n_lat, 256)
    kc = _row_tile(n_lat, 512)
    n_chunks = n_lat // kc
    assert n_chunks >= 2 and n_chunks % 2 == 0
    gw = groups * HEAD_DIM
    ctx_blk = n_lat // n_ctx
    rows = groups * tq
    return pl.pallas_call(
        functools.partial(_global_attn_kernel, groups, tq, kc, n_chunks),
        grid=(hkv, n_lat // tq),
        in_specs=[pl.BlockSpec((tq, gw), lambda h, i: (i, h)),
                  pl.BlockSpec((n_lat, HEAD_DIM), lambda h, i: (0, h)),
                  pl.BlockSpec((n_lat, EXT), lambda h, i: (0, h)),
                  pl.BlockSpec((n_ctx, HEAD_DIM), lambda h, i: (ctx_blk, h)),
                  pl.BlockSpec((n_ctx, EXT), lambda h, i: (ctx_blk, h))],
        out_specs=pl.BlockSpec((tq, gw), lambda h, i: (i, h)),
        out_shape=jax.ShapeDtypeStruct((out_rows, q.shape[1]), BF16),
        scratch_shapes=[pltpu.VMEM((rows, HEAD_DIM), BF16),
                        pltpu.VMEM((rows, LANES), F32),
                        pltpu.VMEM((rows, EXT), F32),
                        pltpu.VMEM((rows, kc), F32),
                        pltpu.VMEM((rows, kc), F32)],
        compiler_params=_cparams("parallel", "arbitrary"),
        name="global_attn",
    )(q, k, v_ext, k, v_ext)


def _window_attn_kernel(groups, tq, n_lat, sink_ref, q_ref, k_ref, v_ref, kc_ref, vc_ref, o_ref,
                        qs_ref, m_ref, acc_ref):
    h = pl.program_id(0)
    i = pl.program_id(1)
    s_idx = pl.program_id(2)
    rows = groups * tq

    @pl.when(s_idx == 0)
    def _():
        _stack_heads(groups, tq, q_ref, qs_ref)
        for g in range(groups):
            m_ref[g * tq:(g + 1) * tq, :] = jnp.full((tq, LANES), sink_ref[h * groups + g], F32)
        lane = lax.broadcasted_iota(jnp.int32, (rows, EXT), 1)
        acc_ref[...] = jnp.where(lane >= HEAD_DIM, 1.0, 0.0).astype(F32)
        _softmax_chunk(qs_ref[...], kc_ref[...], vc_ref[...], m_ref, acc_ref)

    r = lax.broadcasted_iota(jnp.int32, (rows, tq), 0)
    qpos = i * tq + (r & (tq - 1))
    kpos = (i + s_idx - 1) * tq + lax.broadcasted_iota(jnp.int32, (rows, tq), 1)
    ok = (jnp.abs(qpos - kpos) <= WINDOW) & (kpos >= 0) & (kpos < n_lat)
    _softmax_chunk(qs_ref[...], k_ref[...], v_ref[...], m_ref, acc_ref, mask=ok)

    @pl.when(s_idx == pl.num_programs(2) - 1)
    def _():
        _write_heads(groups, tq, acc_ref, o_ref)


def _window_attn(q, k, v_ext, sink, n_lat, n_ctx, out_rows):
    hq = q.shape[1] // HEAD_DIM
    hkv = k.shape[1] // HEAD_DIM
    groups = hq // hkv
    tq = _row_tile(n_lat, 256)
    assert tq >= WINDOW and tq & (tq - 1) == 0
    nb = n_lat // tq
    gw = groups * HEAD_DIM
    ctx_blk = n_lat // n_ctx

    def kv_map(h, i, s):
        return (jnp.clip(i + s - 1, 0, nb - 1), h)

    return pl.pallas_call(
        functools.partial(_window_attn_kernel, groups, tq, n_lat),
        grid=(hkv, nb, 3),
        in_specs=[pl.BlockSpec(memory_space=pltpu.SMEM),
                  pl.BlockSpec((tq, gw), lambda h, i, s: (i, h)),
                  pl.BlockSpec((tq, HEAD_DIM), kv_map),
                  pl.BlockSpec((tq, EXT), kv_map),
                  pl.BlockSpec((n_ctx, HEAD_DIM), lambda h, i, s: (ctx_blk, h)),
                  pl.BlockSpec((n_ctx, EXT), lambda h, i, s: (ctx_blk, h))],
        out_specs=pl.BlockSpec((tq, gw), lambda h, i, s: (i, h)),
        out_shape=jax.ShapeDtypeStruct((out_rows, q.shape[1]), BF16),
        scratch_shapes=[pltpu.VMEM((groups * tq, HEAD_DIM), BF16),
                        pltpu.VMEM((groups * tq, LANES), F32),
                        pltpu.VMEM((groups * tq, EXT), F32)],
        compiler_params=_cparams("parallel", "parallel", "arbitrary"),
        name="window_attn",
    )(sink, q, k, v_ext, k, v_ext)


def _ctx_attn_kernel(groups, has_sink, *refs):
    if has_sink:
        sink_ref, q_ref, k_ref, v_ref, o_ref = refs
    else:
        q_ref, k_ref, v_ref, o_ref = refs
    h = pl.program_id(0)
    for g in range(groups):
        q = q_ref[:, g * HEAD_DIM:(g + 1) * HEAD_DIM]
        s = _nt_dot(q, k_ref[...])
        m = jnp.max(s, axis=-1, keepdims=True)
        if has_sink:
            snk = sink_ref[h * groups + g]
            m = jnp.maximum(m, snk)
        p = jnp.exp2(s - m)
        o = jnp.dot(p.astype(v_ref.dtype), v_ref[...], preferred_element_type=F32)
        l = o[:, HEAD_DIM:]
        if has_sink:
            l = l + jnp.exp2(snk - m)
        o_ref[:, g * HEAD_DIM:(g + 1) * HEAD_DIM] = (o[:, :HEAD_DIM] / l).astype(o_ref.dtype)


def _ctx_attn(q, k, v_ext, sink, n_lat, n_ctx):
    hq = q.shape[1] // HEAD_DIM
    hkv = k.shape[1] // HEAD_DIM
    groups = hq // hkv
    gw = groups * HEAD_DIM
    ctx_blk = n_lat // n_ctx
    has_sink = sink is not None
    in_specs = [pl.BlockSpec((n_ctx, gw), lambda h: (ctx_blk, h)),
                pl.BlockSpec((n_ctx, HEAD_DIM), lambda h: (ctx_blk, h)),
                pl.BlockSpec((n_ctx, EXT), lambda h: (ctx_blk, h))]
    args = [q, k, v_ext]
    if has_sink:
        in_specs.insert(0, pl.BlockSpec(memory_space=pltpu.SMEM))
        args.insert(0, sink)
    return pl.pallas_call(
        functools.partial(_ctx_attn_kernel, groups, has_sink),
        grid=(hkv,),
        in_specs=in_specs,
        out_specs=pl.BlockSpec((n_ctx, gw), lambda h: (0, h)),
        out_shape=jax.ShapeDtypeStruct((n_ctx, q.shape[1]), BF16),
        compiler_params=_cparams("parallel"),
        name="ctx_attn",
    )(*args)


def _na_bias_table(rpb, n_rows):
    kh = min(NA_KH, n_rows)
    assert kh == NA_KH and NA_ROWS == NA_KH and n_rows % NA_ROWS == 0 and n_rows // NA_ROWS >= 2
    half = NA_ROWS // 2
    qr = np.arange(NA_ROWS)
    krel = np.where(qr[:, None] < half, np.arange(-half, 0)[None, :], NA_ROWS + np.arange(half)[None, :])
    krel = np.concatenate([krel, np.broadcast_to(np.arange(NA_ROWS), (NA_ROWS, NA_ROWS))], axis=1)
    nk = krel.shape[1]
    ri = np.zeros((3, NA_ROWS, nk), np.int32)
    rvalid = np.zeros((3, NA_ROWS, nk), bool)
    big = 4 * NA_ROWS
    for var, (r0, rows) in enumerate(((0, big), (NA_ROWS, big), (n_rows - NA_ROWS, n_rows))):
        r = r0 + qr[:, None]
        key = r0 + krel
        rs = np.clip(r - kh // 2, 0, rows - kh)
        rvalid[var] = (key >= rs) & (key < rs + kh)
        ri[var] = np.clip(key - r + NA_KH - 1, 0, 2 * NA_KH - 2)
    qc = np.arange(GRID_W)
    cs = np.clip(qc - NA_KW // 2, 0, GRID_W - NA_KW)
    kc = np.arange(GRID_W)
    cvalid = (kc[None, :] >= cs[:, None]) & (kc[None, :] < cs[:, None] + NA_KW)
    ci = np.clip(kc[None, :] - qc[:, None] + NA_KW - 1, 0, 2 * NA_KW - 2)
    t = jnp.take(rpb * LOG2E, jnp.asarray(ci), axis=2)
    t = t[:, jnp.asarray(ri)]
    valid = rvalid[:, :, :, None, None] & cvalid[None, None, None, :, :]
    t = jnp.where(jnp.asarray(valid)[None], t, NEG)
    t = t.transpose(0, 1, 2, 4, 3, 5)
    return t.reshape(rpb.shape[0], 3, NA_ROWS * GRID_W, nk * GRID_W)


def _na_kernel(nq, q_ref, kp_ref, k0_ref, kn_ref, vp_ref, v0_ref, vn_ref, kc_ref, vc_ref, b_ref, o_ref):
    half = nq // 2
    q = q_ref[...]
    s_nb = jnp.concatenate([_nt_dot(q[:half], kp_ref[...]), _nt_dot(q[half:], kn_ref[...])], axis=0)
    s_nb = s_nb + b_ref[0, 0, :, :half]
    s_own = _nt_dot(q, k0_ref[...]) + b_ref[0, 0, :, half:]
    s_ctx = _nt_dot(q, kc_ref[...])
    m = jnp.maximum(jnp.maximum(s_nb.max(axis=-1, keepdims=True), s_own.max(axis=-1, keepdims=True)),
                    s_ctx.max(axis=-1, keepdims=True))
    p_nb = jnp.exp2(s_nb - m).astype(vp_ref.dtype)
    p_own = jnp.exp2(s_own - m).astype(v0_ref.dtype)
    p_ctx = jnp.exp2(s_ctx - m).astype(vc_ref.dtype)
    o = jnp.concatenate([jnp.dot(p_nb[:half], vp_ref[...], preferred_element_type=F32),
                         jnp.dot(p_nb[half:], vn_ref[...], preferred_element_type=F32)], axis=0)
    o = o + jnp.dot(p_own, v0_ref[...], preferred_element_type=F32)
    o = o + jnp.dot(p_ctx, vc_ref[...], preferred_element_type=F32)
    o_ref[...] = (o[:, :HEAD_DIM] / o[:, HEAD_DIM:]).astype(o_ref.dtype)


def _na_attn(q, k, v_ext, bias, n_lat, n_ctx, out_rows):
    heads = q.shape[1] // HEAD_DIM
    nq = NA_ROWS * GRID_W
    half = nq // 2
    nb = n_lat // nq
    ctx_blk = n_lat // n_ctx

    def prev_map(h, b):
        return (jnp.maximum(2 * b - 1, 0), h)

    def next_map(h, b):
        return (jnp.minimum(2 * b + 2, 2 * nb - 1), h)

    def own_map(h, b):
        return (b, h)

    def ctx_map(h, b):
        return (ctx_blk, h)

    def bias_map(h, b):
        return (h, jnp.where(b == 0, 0, jnp.where(b == nb - 1, 2, 1)), 0, 0)

    return pl.pallas_call(
        functools.partial(_na_kernel, nq),
        grid=(heads, nb),
        in_specs=[pl.BlockSpec((nq, HEAD_DIM), own_map),
                  pl.BlockSpec((half, HEAD_DIM), prev_map), pl.BlockSpec((nq, HEAD_DIM), own_map),
                  pl.BlockSpec((half, HEAD_DIM), next_map),
                  pl.BlockSpec((half, EXT), prev_map), pl.BlockSpec((nq, EXT), own_map),
                  pl.BlockSpec((half, EXT), next_map),
                  pl.BlockSpec((n_ctx, HEAD_DIM), ctx_map), pl.BlockSpec((n_ctx, EXT), ctx_map),
                  pl.BlockSpec((1, 1, nq, half + nq), bias_map)],
        out_specs=pl.BlockSpec((nq, HEAD_DIM), own_map),
        out_shape=jax.ShapeDtypeStruct((out_rows, q.shape[1]), BF16),
        compiler_params=_cparams("parallel", "arbitrary"),
        name="na_attn",
    )(q, k, k, k, v_ext, v_ext, v_ext, k, v_ext, bias)


def _conv_kernel(tr, n_lat_tiles, n_tiles, ch_chunk, prev_ref, cur_ref, next_ref, w_ref, b_ref,
                 g_ref, beta_ref, o_ref, win_ref, y_ref):
    i = pl.program_id(0)
    c = cur_ref.shape[1]
    prev_ok = jnp.logical_and(i != 0, i != n_lat_tiles)
    next_ok = jnp.logical_and(i != n_lat_tiles - 1, i != n_tiles - 1)
    win_ref[0:CONV_HALO, :] = jnp.where(prev_ok, prev_ref[...], 0.0)
    win_ref[CONV_HALO:CONV_HALO + tr, :] = cur_ref[...]
    win_ref[CONV_HALO + tr:, :] = jnp.where(next_ok, next_ref[...], 0.0)
    off = CONV_HALO - CONV_W // 2
    rc = 64

    def col_body(cc, carry):
        c0 = pl.multiple_of(cc * ch_chunk, ch_chunk)
        for r0 in range(0, tr, rc):
            acc = jnp.zeros((rc, ch_chunk), F32)
            for t in range(CONV_W):
                acc = acc + win_ref[r0 + off + t:r0 + off + t + rc, pl.ds(c0, ch_chunk)] * \
                    w_ref[t:t + 1, pl.ds(c0, ch_chunk)]
            y_ref[r0:r0 + rc, pl.ds(c0, ch_chunk)] = acc + b_ref[:, pl.ds(c0, ch_chunk)]
        return carry

    lax.fori_loop(0, c // ch_chunk, col_body, 0)
    y = y_ref[...]
    mu = jnp.mean(y, axis=-1, keepdims=True)
    var = jnp.mean(jnp.square(y - mu), axis=-1, keepdims=True)
    yn = (y - mu) * lax.rsqrt(var + EPS) * g_ref[...] + beta_ref[...]
    o_ref[...] = (yn * _sigmoid(yn)).astype(o_ref.dtype)


def _conformer_conv(u, dw_w, dw_b, ln_g, ln_b, rows, n_lat):
    c = u.shape[1]
    tr = 256
    assert n_lat % tr == 0 and rows % tr == 0
    hb = tr // CONV_HALO
    n_tiles = rows // tr
    last_halo = rows // CONV_HALO - 1
    ch_chunk = math.gcd(c, 256)
    return pl.pallas_call(
        functools.partial(_conv_kernel, tr, n_lat // tr, n_tiles, ch_chunk),
        grid=(n_tiles,),
        in_specs=[pl.BlockSpec((CONV_HALO, c), lambda i: (jnp.maximum(i * hb - 1, 0), 0)),
                  pl.BlockSpec((tr, c), lambda i: (i, 0)),
                  pl.BlockSpec((CONV_HALO, c), lambda i: (jnp.minimum((i + 1) * hb, last_halo), 0)),
                  pl.BlockSpec((CONV_W, c), lambda i: (0, 0)),
                  pl.BlockSpec((1, c), lambda i: (0, 0)),
                  pl.BlockSpec((1, c), lambda i: (0, 0)),
                  pl.BlockSpec((1, c), lambda i: (0, 0))],
        out_specs=pl.BlockSpec((tr, c), lambda i: (i, 0)),
        out_shape=jax.ShapeDtypeStruct((rows, c), BF16),
        scratch_shapes=[pltpu.VMEM((tr + 2 * CONV_HALO, c), F32),
                        pltpu.VMEM((tr, c), F32)],
        compiler_params=_cparams("parallel"),
        name="conformer_conv",
    )(u, u, u, dw_w, dw_b.reshape(1, c), ln_g.reshape(1, c), ln_b.reshape(1, c))


R_EID, R_WT, R_RANK = 0, 2, 4


def _router_kernel(n_groups, per_group, tm, h_ref, w_ref, info_ref, cnt_ref, carry_ref):
    @pl.when(pl.program_id(0) == 0)
    def _():
        carry_ref[...] = jnp.zeros(carry_ref.shape, F32)

    lg = jnp.dot(h_ref[...], w_ref[...], preferred_element_type=F32)
    lane = lax.broadcasted_iota(jnp.int32, lg.shape, 1).astype(F32)
    is_g = lane < n_groups
    gl = jnp.where(is_g, lg, NEG)
    mg = jnp.max(gl, axis=-1, keepdims=True)
    gsel = jnp.min(jnp.where(gl == mg, lane, float(LANES)), axis=-1, keepdims=True)
    gsum = jnp.sum(jnp.where(is_g, jnp.exp(gl - mg), 0.0), axis=-1, keepdims=True)
    lo = n_groups + gsel * per_group
    el = jnp.where((lane >= lo) & (lane < lo + per_group), lg, NEG)
    v1 = jnp.max(el, axis=-1, keepdims=True)
    i1 = jnp.min(jnp.where(el == v1, lane, float(LANES)), axis=-1, keepdims=True)
    el2 = jnp.where(lane == i1, NEG, el)
    v2 = jnp.max(el2, axis=-1, keepdims=True)
    i2 = jnp.min(jnp.where(el2 == v2, lane, float(LANES)), axis=-1, keepdims=True)
    e2 = jnp.exp(v2 - v1)
    w1 = 1.0 / ((1.0 + e2) * gsum)
    w2 = e2 * w1
    hit1 = lane == i1
    hit2 = lane == i2
    cnt = jnp.where(hit1 | hit2, 1.0, 0.0)
    tri = (lax.broadcasted_iota(jnp.int32, (tm, tm), 1) < lax.broadcasted_iota(jnp.int32, (tm, tm), 0))
    before = jnp.dot(jnp.where(tri, 1.0, 0.0).astype(BF16), cnt.astype(BF16), preferred_element_type=F32)
    tot = carry_ref[0:1, :] + before
    r1 = jnp.sum(jnp.where(hit1, tot, 0.0), axis=-1, keepdims=True)
    r2 = jnp.sum(jnp.where(hit2, tot, 0.0), axis=-1, keepdims=True)
    carry_ref[...] = carry_ref[...] + jnp.sum(cnt, axis=0, keepdims=True)
    info = jnp.zeros(lg.shape, F32)
    for col, val in ((R_EID, i1 - n_groups), (R_EID + 1, i2 - n_groups), (R_WT, w1), (R_WT + 1, w2),
                     (R_RANK, r1), (R_RANK + 1, r2)):
        info = jnp.where(lane == col, val, info)
    info_ref[...] = info
    cnt_ref[...] = carry_ref[...]


def _router(h, w_r, n_groups, per_group, rows):
    d = h.shape[1]
    tm = _row_tile(rows, 512)
    return pl.pallas_call(
        functools.partial(_router_kernel, n_groups, per_group, tm),
        grid=(rows // tm,),
        in_specs=[pl.BlockSpec((tm, d), lambda i: (i, 0)),
                  pl.BlockSpec((d, LANES), lambda i: (0, 0))],
        out_specs=[pl.BlockSpec((tm, LANES), lambda i: (i, 0)),
                   pl.BlockSpec((8, LANES), lambda i: (0, 0))],
        out_shape=[jax.ShapeDtypeStruct((rows, LANES), F32),
                   jax.ShapeDtypeStruct((8, LANES), F32)],
        scratch_shapes=[pltpu.VMEM((8, LANES), F32)],
        compiler_params=_cparams("arbitrary"),
        name="moe_router",
    )(h, w_r)


def _cast_on_expert_change(te_ref, pairs):
    i = pl.program_id(0)
    prev = te_ref[jnp.maximum(i - 1, 0)]

    @pl.when(jnp.logical_or(i == 0, te_ref[i] != prev))
    def _():
        for w_ref, wb_ref in pairs:
            wb_ref[...] = w_ref[0, 0].astype(wb_ref.dtype)


def _expert_up_kernel(te_ref, tv_ref, x_ref, wg_ref, wu_ref, o_ref, wgb_ref, wub_ref):
    i = pl.program_id(0)
    _cast_on_expert_change(te_ref, ((wg_ref, wgb_ref), (wu_ref, wub_ref)))

    @pl.when(tv_ref[i] > 0)
    def _():
        x = x_ref[...]
        hg = jnp.dot(x, wgb_ref[...], preferred_element_type=F32)
        hu = jnp.dot(x, wub_ref[...], preferred_element_type=F32)
        o_ref[...] = ((hg * _sigmoid(hg)) * hu).astype(o_ref.dtype)

    @pl.when(tv_ref[i] == 0)
    def _():
        o_ref[...] = jnp.zeros(o_ref.shape, o_ref.dtype)


def _expert_down_kernel(te_ref, tv_ref, a_ref, wd_ref, o_ref, wdb_ref):
    i = pl.program_id(0)
    _cast_on_expert_change(te_ref, ((wd_ref, wdb_ref),))

    @pl.when(tv_ref[i] > 0)
    def _():
        o_ref[...] = jnp.dot(a_ref[...], wdb_ref[...], preferred_element_type=F32).astype(o_ref.dtype)

    @pl.when(tv_ref[i] == 0)
    def _():
        o_ref[...] = jnp.zeros(o_ref.shape, o_ref.dtype)


def _expert_ffn(xs, tile_e, tile_valid, w_g, w_u, w_d, li, tm):
    p, d = xs.shape
    f = w_g.shape[3]

    def wmap(i, te, tv):
        return (li, te[i], 0, 0)

    def rmap(i, te, tv):
        return (i, 0)

    act = pl.pallas_call(
        _expert_up_kernel,
        grid_spec=pltpu.PrefetchScalarGridSpec(
            num_scalar_prefetch=2,
            grid=(p // tm,),
            in_specs=[pl.BlockSpec((tm, d), rmap),
                      pl.BlockSpec((1, 1, d, f), wmap),
                      pl.BlockSpec((1, 1, d, f), wmap)],
            out_specs=pl.BlockSpec((tm, f), rmap),
            scratch_shapes=[pltpu.VMEM((d, f), BF16), pltpu.VMEM((d, f), BF16)]),
        out_shape=jax.ShapeDtypeStruct((p, f), BF16),
        compiler_params=_cparams("arbitrary"),
        name="moe_up",
    )(tile_e, tile_valid, xs, w_g, w_u)
    return pl.pallas_call(
        _expert_down_kernel,
        grid_spec=pltpu.PrefetchScalarGridSpec(
            num_scalar_prefetch=2,
            grid=(p // tm,),
            in_specs=[pl.BlockSpec((tm, f), rmap),
                      pl.BlockSpec((1, 1, f, d), wmap)],
            out_specs=pl.BlockSpec((tm, d), rmap),
            scratch_shapes=[pltpu.VMEM((f, d), BF16)]),
        out_shape=jax.ShapeDtypeStruct((p, d), BF16),
        compiler_params=_cparams("arbitrary"),
        name="moe_down",
    )(tile_e, tile_valid, act, w_d)


def _combine_kernel(n_lat, tm, z_ref, ya_ref, yb_ref, info_ref, gl_ref, gc_ref, o_ref):
    rows = pl.program_id(0) * tm + lax.broadcasted_iota(jnp.int32, (tm, 1), 0)
    gate = jnp.where(rows < n_lat, gl_ref[0], gc_ref[0])
    w1 = info_ref[:, R_WT:R_WT + 1]
    w2 = info_ref[:, R_WT + 1:R_WT + 2]
    y = w1 * ya_ref[...].astype(F32) + w2 * yb_ref[...].astype(F32)
    o_ref[...] = z_ref[...] + gate * y


def _combine(z, y2, info, mods, gate_idx, rows, n_lat):
    d = z.shape[1]
    tm = _row_tile(rows, 256)
    nt = rows // tm
    return pl.pallas_call(
        functools.partial(_combine_kernel, n_lat, tm),
        grid=(nt,),
        in_specs=[pl.BlockSpec((tm, d), lambda i: (i, 0)),
                  pl.BlockSpec((tm, d), lambda i: (i, 0)),
                  pl.BlockSpec((tm, d), lambda i: (i + nt, 0)),
                  pl.BlockSpec((tm, LANES), lambda i: (i, 0)),
                  pl.BlockSpec((1, 1, d), lambda i: (gate_idx, 0, 0)),
                  pl.BlockSpec((1, 1, d), lambda i: (N_MOD + gate_idx, 0, 0))],
        out_specs=pl.BlockSpec((tm, d), lambda i: (i, 0)),
        out_shape=jax.ShapeDtypeStruct((rows, d), F32),
        compiler_params=_cparams("parallel"),
        name="moe_combine",
    )(z, y2, y2, info, mods, mods)


def _rows_of(x, idx):
    return x.at[idx].get(mode="promise_in_bounds", unique_indices=False)


def _moe(h2, z, mods, w_r, w_g, w_u, w_d, li, n_groups, per_group, rows, n_lat):
    n_exp = n_groups * per_group
    tm = MOE_TILE
    info, cnt = _router(h2, w_r, n_groups, per_group, rows)
    eid = info[:, R_EID:R_EID + EXPERT_TOPK].astype(jnp.int32)
    rank = info[:, R_RANK:R_RANK + EXPERT_TOPK].astype(jnp.int32)
    counts = cnt[0, n_groups:n_groups + n_exp].astype(jnp.int32)
    experts = jnp.arange(n_exp, dtype=jnp.int32)
    padded = ((counts + tm - 1) // tm) * tm
    ends = jnp.sum(jnp.where(experts[None, :] <= experts[:, None], padded[None, :], 0), axis=1)
    starts = ends - padded
    pos = jnp.sum(jnp.where(eid[:, :, None] == experts[None, None, :], starts[None, None, :], 0), axis=2) + rank
    pos = pos.T.reshape(-1)
    n_pairs = rows * EXPERT_TOPK
    n_tiles = (n_pairs + n_exp * (tm - 1)) // tm + 1
    p = n_tiles * tm
    pair_token = jnp.arange(n_pairs, dtype=jnp.int32) % rows
    row_token = jnp.zeros((p,), jnp.int32).at[pos].set(pair_token, mode="promise_in_bounds",
                                                      unique_indices=True)
    tile_start = jnp.arange(n_tiles, dtype=jnp.int32) * tm
    tile_e = jnp.minimum(jnp.sum((ends[None, :] <= tile_start[:, None]).astype(jnp.int32), axis=1), n_exp - 1)
    tile_valid = (tile_start < ends[-1]).astype(jnp.int32)
    xs = _rows_of(h2, row_token)
    ys = _expert_ffn(xs, tile_e, tile_valid, w_g, w_u, w_d, li, tm)
    return _combine(z, _rows_of(ys, pos), info, mods, 5, rows, n_lat)


def _rope_tables(n_lat, n_ctx):
    t = jnp.arange(n_lat)
    row = (t // GRID_W).astype(F32)
    col = (t % GRID_W).astype(F32)
    inv = ROPE_THETA ** (-jnp.arange(ROPE_PAIRS, dtype=F32) / ROPE_PAIRS)
    ar = row[:, None] * inv
    ac = col[:, None] * inv
    cos = jnp.concatenate([jnp.cos(ar), jnp.cos(ar), jnp.cos(ac), jnp.cos(ac)], axis=1)
    sin = jnp.concatenate([-jnp.sin(ar), jnp.sin(ar), -jnp.sin(ac), jnp.sin(ac)], axis=1)
    cos = jnp.concatenate([cos, jnp.ones((n_ctx, HEAD_DIM), F32)], axis=0)
    sin = jnp.concatenate([sin, jnp.zeros((n_ctx, HEAD_DIM), F32)], axis=0)
    return cos, sin


def _with_ctx(o_lat, o_ctx, n_lat):
    return lax.dynamic_update_slice(o_lat, o_ctx, (n_lat, 0))


def kernel(x, c, ctx, c_ctx, ada_down, ada_up, ada_bias, norm1_g, norm2_g, w_in_ab, w_out_ab, qnorm_a,
           knorm_a, sink_b, w_in_cd, w_out_cd, rpb_c, dw_w, dw_b, ln_d_g, ln_d_b, router_group,
           router_expert, w_gate, w_up, w_down, final_g):
    assert x.shape[0] == 1 and ctx.shape[0] == 1
    n_lat, d = x.shape[1], x.shape[2]
    n_ctx = ctx.shape[1]
    n_all = n_lat + n_ctx
    depth = ada_down.shape[0]
    assert n_lat % n_ctx == 0 and n_ctx % 128 == 0
    n_groups, per_group = router_expert.shape[1], router_expert.shape[3]
    n_mix = d // HEAD_DIM
    a_w = (n_mix // 2) * HEAD_DIM
    akv_w = (n_mix // 8) * HEAD_DIM
    c_w = (n_mix // 2) * HEAD_DIM
    d_ch = d // 2
    qscale = HEAD_DIM ** -0.5 * LOG2E

    z = jnp.concatenate([x[0], ctx[0]], axis=0)
    cond = jnp.zeros((8, d), F32).at[0].set(c[0]).at[1].set(c_ctx)
    mods_all = _modulation_all(cond, ada_down, ada_up, ada_bias)
    tabs = _rope_tables(n_lat, n_ctx)

    for layer in range(depth):
        need_ctx = layer < depth - 1
        rows = n_all if need_ctx else n_lat
        mods = mods_all[layer, :2].reshape(2 * N_MOD, 1, d)
        h = _norm_mod(z, norm1_g[layer], mods, 0, 1, n_all, n_lat, BF16)
        i = layer // 2
        if layer % 2 == 0:
            w = w_in_ab
            sink = sink_b[i] * LOG2E
            col = 0
            qa = _proj_in(h, w, i, col, a_w, n_all, gain=qnorm_a[i], rope_tabs=tabs, scale=qscale)
            col += a_w
            ka = _proj_in(h, w, i, col, akv_w, n_all, gain=knorm_a[i], rope_tabs=tabs)
            col += akv_w
            va = _proj_in(h, w, i, col, akv_w, n_all, ones_ext=True)
            col += akv_w
            qb = _proj_in(h, w, i, col, a_w, n_all, rope_tabs=tabs, scale=qscale)
            col += a_w
            kb = _proj_in(h, w, i, col, akv_w, n_all, rope_tabs=tabs)
            col += akv_w
            vb = _proj_in(h, w, i, col, akv_w, n_all, ones_ext=True)
            o1 = _global_attn(qa, ka, va, n_lat, n_ctx, rows)
            o2 = _window_attn(qb, kb, vb, sink, n_lat, n_ctx, rows)
            if need_ctx:
                o1 = _with_ctx(o1, _ctx_attn(qa, ka, va, None, n_lat, n_ctx), n_lat)
                o2 = _with_ctx(o2, _ctx_attn(qb, kb, vb, sink, n_lat, n_ctx), n_lat)
            w_out = w_out_ab
        else:
            w = w_in_cd
            qc = _proj_in(h, w, i, 0, c_w, n_all, scale=qscale)
            kc = _proj_in(h, w, i, c_w, c_w, n_all)
            vc = _proj_in(h, w, i, 2 * c_w, c_w, n_all, ones_ext=True)
            u = _proj_glu(h, w, i, 3 * c_w, d_ch, n_all)
            bias = _na_bias_table(rpb_c[i], n_lat // GRID_W)
            o1 = _na_attn(qc, kc, vc, bias, n_lat, n_ctx, rows)
            if need_ctx:
                o1 = _with_ctx(o1, _ctx_attn(qc, kc, vc, None, n_lat, n_ctx), n_lat)
            o2 = _conformer_conv(u, dw_w[i], dw_b[i], ln_d_g[i], ln_d_b[i], rows, n_lat)
            w_out = w_out_cd
        z = _proj_out(o1, o2, w_out, i, z, mods, 2, rows, n_lat)
        h2 = _norm_mod(z, norm2_g[layer], mods, 3, 4, rows, n_lat, BF16)
        w_r = jnp.concatenate(
            [router_group[layer], router_expert[layer].transpose(1, 0, 2).reshape(d, n_groups * per_group)],
            axis=1)
        w_r = jnp.pad(w_r, ((0, 0), (0, LANES - w_r.shape[1]))).astype(BF16)
        z = _moe(h2, z, mods, w_r, w_gate, w_up, w_down, layer, n_groups, per_group, rows, n_lat)
    out = _norm_mod(z, final_g, None, 0, 0, n_lat, n_lat, F32)
    return out[None]
```

```python
import functools
import math

import numpy as np
import jax
import jax.numpy as jnp
from jax import lax
from jax.experimental import pallas as pl
from jax.experimental.pallas import tpu as pltpu

HEAD_DIM = 128
GRID_W = 64
ROPE_PAIRS = HEAD_DIM // 4
ROPE_THETA = 10000.0
WINDOW = 128
NA_KH = 8
NA_KW = 16
CONV_W = 31
N_MOD = 6
EXPERT_TOPK = 2
EPS = 1e-6
NEG = -1e30
LOG2E = math.log2(math.e)
LANES = 128
VMEM_LIMIT = 56 * 1024 * 1024
NA_ROWS = 8
CONV_HALO = 16
EXT = 2 * HEAD_DIM
MOE_TILE = 256
MOE_PARTS = 2

BF16 = jnp.bfloat16
F32 = jnp.float32


def _cparams(*sem):
    return pltpu.CompilerParams(dimension_semantics=sem, vmem_limit_bytes=VMEM_LIMIT)


def _row_tile(m, cap):
    best = None
    for t in range(LANES, min(m, cap) + 1, LANES):
        if m % t == 0:
            best = t
    assert best is not None, (m, cap)
    return best


def _sigmoid(x):
    return 1.0 / (1.0 + jnp.exp(-x))


def _mod_down_kernel(cond_ref, down_ref, o_ref):
    cnd = cond_ref[...]
    a = cnd * _sigmoid(cnd)
    o_ref[0] = jnp.dot(a, down_ref[0], preferred_element_type=F32, precision=lax.Precision.HIGHEST)


def _mod_up_kernel(t_ref, up_ref, b_ref, o_ref):
    o_ref[0] = jnp.dot(t_ref[0], up_ref[0], preferred_element_type=F32,
                       precision=lax.Precision.HIGHEST) + b_ref[0]


def _modulation_all(cond, ada_down, ada_up, ada_bias):
    depth, d, r = ada_down.shape
    n = ada_up.shape[2]
    t = pl.pallas_call(
        _mod_down_kernel,
        grid=(depth,),
        in_specs=[pl.BlockSpec((8, d), lambda l: (0, 0)),
                  pl.BlockSpec((1, d, r), lambda l: (l, 0, 0))],
        out_specs=pl.BlockSpec((1, 8, r), lambda l: (l, 0, 0)),
        out_shape=jax.ShapeDtypeStruct((depth, 8, r), F32),
        compiler_params=_cparams("arbitrary"),
        name="mod_down",
    )(cond, ada_down)
    tn = _row_tile(n, 2048)
    return pl.pallas_call(
        _mod_up_kernel,
        grid=(depth, n // tn),
        in_specs=[pl.BlockSpec((1, 8, r), lambda l, j: (l, 0, 0)),
                  pl.BlockSpec((1, r, tn), lambda l, j: (l, 0, j)),
                  pl.BlockSpec((1, 1, tn), lambda l, j: (l, 0, j))],
        out_specs=pl.BlockSpec((1, 8, tn), lambda l, j: (l, 0, j)),
        out_shape=jax.ShapeDtypeStruct((depth, 8, n), F32),
        compiler_params=_cparams("arbitrary", "arbitrary"),
        name="mod_up",
    )(t, ada_up, ada_bias.reshape(depth, 1, n))


def _norm_mod_kernel(n_lat, tm, modulated, z_ref, g_ref, *rest):
    if modulated:
        sh_l, sc_l, sh_c, sc_c, o_ref = rest
    else:
        (o_ref,) = rest
    x = z_ref[...]
    r = lax.rsqrt(jnp.mean(x * x, axis=-1, keepdims=True) + EPS)
    if modulated:
        is_lat = pl.program_id(0) * tm < n_lat
        shift = jnp.where(is_lat, sh_l[0], sh_c[0])
        scale = jnp.where(is_lat, sc_l[0], sc_c[0])
        h = (x * r) * (g_ref[...] * (1.0 + scale)) + shift
    else:
        h = x * r * g_ref[...]
    o_ref[...] = h.astype(o_ref.dtype)


def _norm_mod(z, g, mods, a, b, rows, n_lat, out_dtype):
    d = z.shape[1]
    tm = _row_tile(math.gcd(rows, n_lat), 256)
    modulated = mods is not None
    in_specs = [pl.BlockSpec((tm, d), lambda i: (i, 0)),
                pl.BlockSpec((1, d), lambda i: (0, 0))]
    args = [z, g.reshape(1, d)]
    if modulated:
        for idx in (a, b, N_MOD + a, N_MOD + b):
            in_specs.append(pl.BlockSpec((1, 1, d), lambda i, idx=idx: (idx, 0, 0)))
            args.append(mods)
    return pl.pallas_call(
        functools.partial(_norm_mod_kernel, n_lat, tm, modulated),
        grid=(rows // tm,),
        in_specs=in_specs,
        out_specs=pl.BlockSpec((tm, d), lambda i: (i, 0)),
        out_shape=jax.ShapeDtypeStruct((rows, d), out_dtype),
        compiler_params=_cparams("parallel"),
        name="norm_mod",
    )(*args)


def _rope_swap(x):
    lane = lax.broadcasted_iota(jnp.int32, x.shape, 1)
    first_half = (lane % (2 * ROPE_PAIRS)) < ROPE_PAIRS
    return jnp.where(first_half, pltpu.roll(x, HEAD_DIM - ROPE_PAIRS, 1), pltpu.roll(x, ROPE_PAIRS, 1))


def _cast_weight_tile(w_ref, wb_ref):
    @pl.when(pl.program_id(1) == 0)
    def _():
        wb_ref[...] = w_ref[0].astype(wb_ref.dtype)


def _proj_in_kernel(norm, rope, scale, ones_ext, heads, a_ref, w_ref, *rest):
    rest = list(rest)
    g_ref = rest.pop(0) if norm else None
    if rope:
        c_ref = rest.pop(0)
        s_ref = rest.pop(0)
    o_ref, wb_ref = rest
    _cast_weight_tile(w_ref, wb_ref)
    acc = jnp.dot(a_ref[...], wb_ref[...], preferred_element_type=F32)
    if not (norm or rope or ones_ext):
        if scale != 1.0:
            acc = acc * scale
        o_ref[...] = acc.astype(o_ref.dtype)
        return
    ow = EXT if ones_ext else HEAD_DIM
    for hh in range(heads):
        x = acc[:, hh * HEAD_DIM:(hh + 1) * HEAD_DIM]
        if norm:
            r = lax.rsqrt(jnp.mean(x * x, axis=-1, keepdims=True) + EPS)
            x = x * r * g_ref[...]
        if rope:
            x = x * c_ref[...] + _rope_swap(x) * s_ref[...]
        if scale != 1.0:
            x = x * scale
        o_ref[:, hh * ow:hh * ow + HEAD_DIM] = x.astype(o_ref.dtype)
        if ones_ext:
            o_ref[:, hh * ow + HEAD_DIM:(hh + 1) * ow] = jnp.ones(x.shape, o_ref.dtype)


def _proj_in(h, w, li, col0, width, rows, *, gain=None, rope_tabs=None, scale=1.0, ones_ext=False):
    k = h.shape[1]
    tm = _row_tile(rows, 1280)
    tn = math.gcd(math.gcd(width, 512), col0) if col0 else math.gcd(width, 512)
    norm = gain is not None
    rope = rope_tabs is not None
    in_specs = [pl.BlockSpec((tm, k), lambda j, i: (i, 0)),
                pl.BlockSpec((1, k, tn), lambda j, i: (li, 0, col0 // tn + j))]
    args = [h, w]
    if norm:
        in_specs.append(pl.BlockSpec((1, HEAD_DIM), lambda j, i: (0, 0)))
        args.append(gain.reshape(1, HEAD_DIM))
    if rope:
        for tab in rope_tabs:
            in_specs.append(pl.BlockSpec((tm, HEAD_DIM), lambda j, i: (i, 0)))
            args.append(tab)
    mult = 2 if ones_ext else 1
    return pl.pallas_call(
        functools.partial(_proj_in_kernel, norm, rope, scale, ones_ext, tn // HEAD_DIM),
        grid=(width // tn, rows // tm),
        in_specs=in_specs,
        out_specs=pl.BlockSpec((tm, mult * tn), lambda j, i: (i, j)),
        out_shape=jax.ShapeDtypeStruct((rows, mult * width), BF16),
        scratch_shapes=[pltpu.VMEM((k, tn), BF16)],
        compiler_params=_cparams("parallel", "arbitrary"),
        name="proj_in",
    )(*args)


def _proj_glu_kernel(a_ref, wa_ref, wg_ref, o_ref, wab_ref, wgb_ref):
    _cast_weight_tile(wa_ref, wab_ref)
    _cast_weight_tile(wg_ref, wgb_ref)
    a = a_ref[...]
    va = jnp.dot(a, wab_ref[...], preferred_element_type=F32)
    vg = jnp.dot(a, wgb_ref[...], preferred_element_type=F32)
    o_ref[...] = (va * _sigmoid(vg)).astype(o_ref.dtype)


def _proj_glu(h, w, li, col0, width, rows):
    k = h.shape[1]
    tm = _row_tile(rows, 1280)
    tn = math.gcd(math.gcd(width, 256), col0)
    return pl.pallas_call(
        _proj_glu_kernel,
        grid=(width // tn, rows // tm),
        in_specs=[pl.BlockSpec((tm, k), lambda j, i: (i, 0)),
                  pl.BlockSpec((1, k, tn), lambda j, i: (li, 0, col0 // tn + j)),
                  pl.BlockSpec((1, k, tn), lambda j, i: (li, 0, (col0 + width) // tn + j))],
        out_specs=pl.BlockSpec((tm, tn), lambda j, i: (i, j)),
        out_shape=jax.ShapeDtypeStruct((rows, width), F32),
        scratch_shapes=[pltpu.VMEM((k, tn), BF16), pltpu.VMEM((k, tn), BF16)],
        compiler_params=_cparams("parallel", "arbitrary"),
        name="proj_glu",
    )(h, w, w)


def _proj_out_kernel(n_lat, tm, a1_ref, a2_ref, w1_ref, w2_ref, z_ref, gl_ref, gc_ref, o_ref,
                     w1b_ref, w2b_ref):
    _cast_weight_tile(w1_ref, w1b_ref)
    _cast_weight_tile(w2_ref, w2b_ref)
    acc = jnp.dot(a1_ref[...], w1b_ref[...], preferred_element_type=F32)
    acc = acc + jnp.dot(a2_ref[...], w2b_ref[...], preferred_element_type=F32)
    rows = pl.program_id(1) * tm + lax.broadcasted_iota(jnp.int32, (tm, 1), 0)
    gate = jnp.where(rows < n_lat, gl_ref[0], gc_ref[0])
    o_ref[...] = z_ref[...] + gate * acc


def _proj_out(a1, a2, w, li, z, mods, gate_idx, rows, n_lat):
    k1, k2 = a1.shape[1], a2.shape[1]
    assert k1 == k2
    n = w.shape[2]
    tm = _row_tile(rows, 1280)
    tn = math.gcd(n, 512)
    return pl.pallas_call(
        functools.partial(_proj_out_kernel, n_lat, tm),
        grid=(n // tn, rows // tm),
        in_specs=[pl.BlockSpec((tm, k1), lambda j, i: (i, 0)),
                  pl.BlockSpec((tm, k2), lambda j, i: (i, 0)),
                  pl.BlockSpec((1, k1, tn), lambda j, i: (li, 0, j)),
                  pl.BlockSpec((1, k2, tn), lambda j, i: (li, 1, j)),
                  pl.BlockSpec((tm, tn), lambda j, i: (i, j)),
                  pl.BlockSpec((1, 1, tn), lambda j, i: (gate_idx, 0, j)),
                  pl.BlockSpec((1, 1, tn), lambda j, i: (N_MOD + gate_idx, 0, j))],
        out_specs=pl.BlockSpec((tm, tn), lambda j, i: (i, j)),
        out_shape=jax.ShapeDtypeStruct((rows, n), F32),
        scratch_shapes=[pltpu.VMEM((k1, tn), BF16), pltpu.VMEM((k2, tn), BF16)],
        compiler_params=_cparams("parallel", "arbitrary"),
        name="proj_out",
    )(a1, a2, w, w, z, mods, mods)


def _nt_dot(a, b):
    return lax.dot_general(a, b, (((1,), (1,)), ((), ())), preferred_element_type=F32)


def _softmax_pv(s, v_ext, m_ref, acc_ref):
    m_prev = m_ref[...]
    m_new = jnp.maximum(m_prev, jnp.max(s, axis=1, keepdims=True))
    alpha = jnp.exp2(m_prev - m_new)
    p = jnp.exp2(s - jnp.tile(m_new, (1, s.shape[1] // LANES)))
    pv = jnp.dot(p.astype(v_ext.dtype), v_ext, preferred_element_type=F32)
    acc_ref[...] = jnp.tile(alpha, (1, EXT // LANES)) * acc_ref[...] + pv
    m_ref[...] = m_new


def _softmax_chunk(q, k, v_ext, m_ref, acc_ref, mask=None):
    s = _nt_dot(q, k)
    if mask is not None:
        s = jnp.where(mask, s, NEG)
    _softmax_pv(s, v_ext, m_ref, acc_ref)


def _stack_heads(groups, tq, q_ref, qs_ref):
    for g in range(groups):
        qs_ref[g * tq:(g + 1) * tq, :] = q_ref[:, g * HEAD_DIM:(g + 1) * HEAD_DIM]


def _write_heads(groups, tq, acc_ref, o_ref):
    for g in range(groups):
        a = acc_ref[g * tq:(g + 1) * tq, :]
        o_ref[:, g * HEAD_DIM:(g + 1) * HEAD_DIM] = (a[:, :HEAD_DIM] / a[:, HEAD_DIM:]).astype(o_ref.dtype)


def _global_attn_kernel(groups, tq, kc, n_chunks, q_ref, k_ref, v_ref, kc_ref, vc_ref, o_ref,
                        qs_ref, m_ref, acc_ref, s0_ref, s1_ref):
    _stack_heads(groups, tq, q_ref, qs_ref)
    m_ref[...] = jnp.full(m_ref.shape, NEG, F32)
    acc_ref[...] = jnp.zeros(acc_ref.shape, F32)
    _softmax_chunk(qs_ref[...], kc_ref[...], vc_ref[...], m_ref, acc_ref)

    def logits(c):
        return _nt_dot(qs_ref[...], k_ref[pl.ds(pl.multiple_of(c * kc, kc), kc), :])

    def update(s_ref, c):
        _softmax_pv(s_ref[...], v_ref[pl.ds(pl.multiple_of(c * kc, kc), kc), :], m_ref, acc_ref)

    s0_ref[...] = logits(0)

    def body(t, carry):
        c = 2 * t
        s1_ref[...] = logits(c + 1)
        update(s0_ref, c)
        s0_ref[...] = logits(c + 2)
        update(s1_ref, c + 1)
        return carry

    lax.fori_loop(0, n_chunks // 2 - 1, body, 0)
    s1_ref[...] = logits(n_chunks - 1)
    update(s0_ref, n_chunks - 2)
    update(s1_ref, n_chunks - 1)
    _write_heads(groups, tq, acc_ref, o_ref)


def _global_attn(q, k, v_ext, n_lat, n_ctx, out_rows):
    hq = q.shape[1] // HEAD_DIM
    hkv = k.shape[1] // HEAD_DIM
    groups = hq // hkv
    tq = _row_tile(n_lat, 256)
    kc = _row_tile(n_lat, 512)
    n_chunks = n_lat // kc
    assert n_chunks >= 2 and n_chunks % 2 == 0
    gw = groups * HEAD_DIM
    ctx_blk = n_lat // n_ctx
    rows = groups * tq
    return pl.pallas_call(
        functools.partial(_global_attn_kernel, groups, tq, kc, n_chunks),
        grid=(hkv, n_lat // tq),
        in_specs=[pl.BlockSpec((tq, gw), lambda h, i: (i, h)),
                  pl.BlockSpec((n_lat, HEAD_DIM), lambda h, i: (0, h)),
                  pl.BlockSpec((n_lat, EXT), lambda h, i: (0, h)),
                  pl.BlockSpec((n_ctx, HEAD_DIM), lambda h, i: (ctx_blk, h)),
                  pl.BlockSpec((n_ctx, EXT), lambda h, i: (ctx_blk, h))],
        out_specs=pl.BlockSpec((tq, gw), lambda h, i: (i, h)),
        out_shape=jax.ShapeDtypeStruct((out_rows, q.shape[1]), BF16),
        scratch_shapes=[pltpu.VMEM((rows, HEAD_DIM), BF16),
                        pltpu.VMEM((rows, LANES), F32),
                        pltpu.VMEM((rows, EXT), F32),
                        pltpu.VMEM((rows, kc), F32),
                        pltpu.VMEM((rows, kc), F32)],
        compiler_params=_cparams("parallel", "arbitrary"),
        name="global_attn",
    )(q, k, v_ext, k, v_ext)


def _window_attn_kernel(groups, tq, span, n_lat, sink_ref, q_ref, k_ref, v_ref, kc_ref, vc_ref, o_ref,
                        qs_ref, m_ref, acc_ref):
    h = pl.program_id(0)
    i = pl.program_id(1)
    rows = groups * tq
    _stack_heads(groups, tq, q_ref, qs_ref)
    for g in range(groups):
        m_ref[g * tq:(g + 1) * tq, :] = jnp.full((tq, LANES), sink_ref[h * groups + g], F32)
    lane = lax.broadcasted_iota(jnp.int32, (rows, EXT), 1)
    acc_ref[...] = jnp.where(lane >= HEAD_DIM, 1.0, 0.0).astype(F32)
    _softmax_chunk(qs_ref[...], kc_ref[...], vc_ref[...], m_ref, acc_ref)
    start = pl.multiple_of(jnp.clip(i * tq - WINDOW, 0, n_lat - span), WINDOW)
    qpos = i * tq + lax.broadcasted_iota(jnp.int32, (tq, span), 0)
    kpos = start + lax.broadcasted_iota(jnp.int32, (tq, span), 1)
    band = jnp.where(jnp.abs(qpos - kpos) <= WINDOW, 0.0, NEG)
    s = _nt_dot(qs_ref[...], k_ref[pl.ds(start, span), :])
    s = (s.reshape(groups, tq, span) + band[None]).reshape(rows, span)
    _softmax_pv(s, v_ref[pl.ds(start, span), :], m_ref, acc_ref)
    _write_heads(groups, tq, acc_ref, o_ref)


def _window_attn(q, k, v_ext, sink, n_lat, n_ctx, out_rows):
    hq = q.shape[1] // HEAD_DIM
    hkv = k.shape[1] // HEAD_DIM
    groups = hq // hkv
    tq = _row_tile(n_lat, 256)
    span = tq + 2 * WINDOW
    assert tq % WINDOW == 0 and n_lat >= span
    gw = groups * HEAD_DIM
    ctx_blk = n_lat // n_ctx
    rows = groups * tq
    return pl.pallas_call(
        functools.partial(_window_attn_kernel, groups, tq, span, n_lat),
        grid=(hkv, n_lat // tq),
        in_specs=[pl.BlockSpec(memory_space=pltpu.SMEM),
                  pl.BlockSpec((tq, gw), lambda h, i: (i, h)),
                  pl.BlockSpec((n_lat, HEAD_DIM), lambda h, i: (0, h)),
                  pl.BlockSpec((n_lat, EXT), lambda h, i: (0, h)),
                  pl.BlockSpec((n_ctx, HEAD_DIM), lambda h, i: (ctx_blk, h)),
                  pl.BlockSpec((n_ctx, EXT), lambda h, i: (ctx_blk, h))],
        out_specs=pl.BlockSpec((tq, gw), lambda h, i: (i, h)),
        out_shape=jax.ShapeDtypeStruct((out_rows, q.shape[1]), BF16),
        scratch_shapes=[pltpu.VMEM((rows, HEAD_DIM), BF16),
                        pltpu.VMEM((rows, LANES), F32),
                        pltpu.VMEM((rows, EXT), F32)],
        compiler_params=_cparams("parallel", "arbitrary"),
        name="window_attn",
    )(sink, q, k, v_ext, k, v_ext)


def _ctx_attn_kernel(groups, has_sink, *refs):
    if has_sink:
        sink_ref, q_ref, k_ref, v_ref, o_ref = refs
    else:
        q_ref, k_ref, v_ref, o_ref = refs
    h = pl.program_id(0)
    for g in range(groups):
        q = q_ref[:, g * HEAD_DIM:(g + 1) * HEAD_DIM]
        s = _nt_dot(q, k_ref[...])
        m = jnp.max(s, axis=-1, keepdims=True)
        if has_sink:
            snk = sink_ref[h * groups + g]
            m = jnp.maximum(m, snk)
        p = jnp.exp2(s - m)
        o = jnp.dot(p.astype(v_ref.dtype), v_ref[...], preferred_element_type=F32)
        l = o[:, HEAD_DIM:]
        if has_sink:
            l = l + jnp.exp2(snk - m)
        o_ref[:, g * HEAD_DIM:(g + 1) * HEAD_DIM] = (o[:, :HEAD_DIM] / l).astype(o_ref.dtype)


def _ctx_attn(q, k, v_ext, sink, n_lat, n_ctx):
    hq = q.shape[1] // HEAD_DIM
    hkv = k.shape[1] // HEAD_DIM
    groups = hq // hkv
    gw = groups * HEAD_DIM
    ctx_blk = n_lat // n_ctx
    has_sink = sink is not None
    in_specs = [pl.BlockSpec((n_ctx, gw), lambda h: (ctx_blk, h)),
                pl.BlockSpec((n_ctx, HEAD_DIM), lambda h: (ctx_blk, h)),
                pl.BlockSpec((n_ctx, EXT), lambda h: (ctx_blk, h))]
    args = [q, k, v_ext]
    if has_sink:
        in_specs.insert(0, pl.BlockSpec(memory_space=pltpu.SMEM))
        args.insert(0, sink)
    return pl.pallas_call(
        functools.partial(_ctx_attn_kernel, groups, has_sink),
        grid=(hkv,),
        in_specs=in_specs,
        out_specs=pl.BlockSpec((n_ctx, gw), lambda h: (0, h)),
        out_shape=jax.ShapeDtypeStruct((n_ctx, q.shape[1]), BF16),
        compiler_params=_cparams("parallel"),
        name="ctx_attn",
    )(*args)


def _na_bias_table(rpb, n_rows):
    kh = min(NA_KH, n_rows)
    assert kh == NA_KH and NA_ROWS == NA_KH and n_rows % NA_ROWS == 0 and n_rows // NA_ROWS >= 2
    half = NA_ROWS // 2
    qr = np.arange(NA_ROWS)
    krel = np.where(qr[:, None] < half, np.arange(-half, 0)[None, :], NA_ROWS + np.arange(half)[None, :])
    krel = np.concatenate([krel, np.broadcast_to(np.arange(NA_ROWS), (NA_ROWS, NA_ROWS))], axis=1)
    nk = krel.shape[1]
    ri = np.zeros((3, NA_ROWS, nk), np.int32)
    rvalid = np.zeros((3, NA_ROWS, nk), bool)
    big = 4 * NA_ROWS
    for var, (r0, rows) in enumerate(((0, big), (NA_ROWS, big), (n_rows - NA_ROWS, n_rows))):
        r = r0 + qr[:, None]
        key = r0 + krel
        rs = np.clip(r - kh // 2, 0, rows - kh)
        rvalid[var] = (key >= rs) & (key < rs + kh)
        ri[var] = np.clip(key - r + NA_KH - 1, 0, 2 * NA_KH - 2)
    qc = np.arange(GRID_W)
    cs = np.clip(qc - NA_KW // 2, 0, GRID_W - NA_KW)
    kc = np.arange(GRID_W)
    cvalid = (kc[None, :] >= cs[:, None]) & (kc[None, :] < cs[:, None] + NA_KW)
    ci = np.clip(kc[None, :] - qc[:, None] + NA_KW - 1, 0, 2 * NA_KW - 2)
    t = jnp.take(rpb * LOG2E, jnp.asarray(ci), axis=2)
    t = t[:, jnp.asarray(ri)]
    valid = rvalid[:, :, :, None, None] & cvalid[None, None, None, :, :]
    t = jnp.where(jnp.asarray(valid)[None], t, NEG)
    t = t.transpose(0, 1, 2, 4, 3, 5)
    return t.reshape(rpb.shape[0], 3, NA_ROWS * GRID_W, nk * GRID_W)


def _na_kernel(nq, q_ref, kp_ref, k0_ref, kn_ref, vp_ref, v0_ref, vn_ref, kc_ref, vc_ref, b_ref, o_ref):
    half = nq // 2
    q = q_ref[...]
    s_nb = jnp.concatenate([_nt_dot(q[:half], kp_ref[...]), _nt_dot(q[half:], kn_ref[...])], axis=0)
    s_nb = s_nb + b_ref[0, 0, :, :half]
    s_own = _nt_dot(q, k0_ref[...]) + b_ref[0, 0, :, half:]
    s_ctx = _nt_dot(q, kc_ref[...])
    m = jnp.maximum(jnp.maximum(s_nb.max(axis=-1, keepdims=True), s_own.max(axis=-1, keepdims=True)),
                    s_ctx.max(axis=-1, keepdims=True))
    p_nb = jnp.exp2(s_nb - m).astype(vp_ref.dtype)
    p_own = jnp.exp2(s_own - m).astype(v0_ref.dtype)
    p_ctx = jnp.exp2(s_ctx - m).astype(vc_ref.dtype)
    o = jnp.concatenate([jnp.dot(p_nb[:half], vp_ref[...], preferred_element_type=F32),
                         jnp.dot(p_nb[half:], vn_ref[...], preferred_element_type=F32)], axis=0)
    o = o + jnp.dot(p_own, v0_ref[...], preferred_element_type=F32)
    o = o + jnp.dot(p_ctx, vc_ref[...], preferred_element_type=F32)
    o_ref[...] = (o[:, :HEAD_DIM] / o[:, HEAD_DIM:]).astype(o_ref.dtype)


def _na_attn(q, k, v_ext, bias, n_lat, n_ctx, out_rows):
    heads = q.shape[1] // HEAD_DIM
    nq = NA_ROWS * GRID_W
    half = nq // 2
    nb = n_lat // nq
    ctx_blk = n_lat // n_ctx

    def prev_map(h, b):
        return (jnp.maximum(2 * b - 1, 0), h)

    def next_map(h, b):
        return (jnp.minimum(2 * b + 2, 2 * nb - 1), h)

    def own_map(h, b):
        return (b, h)

    def ctx_map(h, b):
        return (ctx_blk, h)

    def bias_map(h, b):
        return (h, jnp.where(b == 0, 0, jnp.where(b == nb - 1, 2, 1)), 0, 0)

    return pl.pallas_call(
        functools.partial(_na_kernel, nq),
        grid=(heads, nb),
        in_specs=[pl.BlockSpec((nq, HEAD_DIM), own_map),
                  pl.BlockSpec((half, HEAD_DIM), prev_map), pl.BlockSpec((nq, HEAD_DIM), own_map),
                  pl.BlockSpec((half, HEAD_DIM), next_map),
                  pl.BlockSpec((half, EXT), prev_map), pl.BlockSpec((nq, EXT), own_map),
                  pl.BlockSpec((half, EXT), next_map),
                  pl.BlockSpec((n_ctx, HEAD_DIM), ctx_map), pl.BlockSpec((n_ctx, EXT), ctx_map),
                  pl.BlockSpec((1, 1, nq, half + nq), bias_map)],
        out_specs=pl.BlockSpec((nq, HEAD_DIM), own_map),
        out_shape=jax.ShapeDtypeStruct((out_rows, q.shape[1]), BF16),
        compiler_params=_cparams("parallel", "arbitrary"),
        name="na_attn",
    )(q, k, k, k, v_ext, v_ext, v_ext, k, v_ext, bias)


def _conv_kernel(tr, n_lat_tiles, n_tiles, ch_chunk, prev_ref, cur_ref, next_ref, w_ref, b_ref,
                 g_ref, beta_ref, o_ref, win_ref, y_ref):
    i = pl.program_id(0)
    c = cur_ref.shape[1]
    prev_ok = jnp.logical_and(i != 0, i != n_lat_tiles)
    next_ok = jnp.logical_and(i != n_lat_tiles - 1, i != n_tiles - 1)
    win_ref[0:CONV_HALO, :] = jnp.where(prev_ok, prev_ref[...], 0.0)
    win_ref[CONV_HALO:CONV_HALO + tr, :] = cur_ref[...]
    win_ref[CONV_HALO + tr:, :] = jnp.where(next_ok, next_ref[...], 0.0)
    off = CONV_HALO - CONV_W // 2
    rc = 64

    def col_body(cc, carry):
        c0 = pl.multiple_of(cc * ch_chunk, ch_chunk)
        for r0 in range(0, tr, rc):
            acc = jnp.zeros((rc, ch_chunk), F32)
            for t in range(CONV_W):
                acc = acc + win_ref[r0 + off + t:r0 + off + t + rc, pl.ds(c0, ch_chunk)] * \
                    w_ref[t:t + 1, pl.ds(c0, ch_chunk)]
            y_ref[r0:r0 + rc, pl.ds(c0, ch_chunk)] = acc + b_ref[:, pl.ds(c0, ch_chunk)]
        return carry

    lax.fori_loop(0, c // ch_chunk, col_body, 0)
    y = y_ref[...]
    mu = jnp.mean(y, axis=-1, keepdims=True)
    var = jnp.mean(jnp.square(y - mu), axis=-1, keepdims=True)
    yn = (y - mu) * lax.rsqrt(var + EPS) * g_ref[...] + beta_ref[...]
    o_ref[...] = (yn * _sigmoid(yn)).astype(o_ref.dtype)


def _conformer_conv(u, dw_w, dw_b, ln_g, ln_b, rows, n_lat):
    c = u.shape[1]
    tr = 256
    assert n_lat % tr == 0 and rows % tr == 0
    hb = tr // CONV_HALO
    n_tiles = rows // tr
    last_halo = rows // CONV_HALO - 1
    ch_chunk = math.gcd(c, 256)
    return pl.pallas_call(
        functools.partial(_conv_kernel, tr, n_lat // tr, n_tiles, ch_chunk),
        grid=(n_tiles,),
        in_specs=[pl.BlockSpec((CONV_HALO, c), lambda i: (jnp.maximum(i * hb - 1, 0), 0)),
                  pl.BlockSpec((tr, c), lambda i: (i, 0)),
                  pl.BlockSpec((CONV_HALO, c), lambda i: (jnp.minimum((i + 1) * hb, last_halo), 0)),
                  pl.BlockSpec((CONV_W, c), lambda i: (0, 0)),
                  pl.BlockSpec((1, c), lambda i: (0, 0)),
                  pl.BlockSpec((1, c), lambda i: (0, 0)),
                  pl.BlockSpec((1, c), lambda i: (0, 0))],
        out_specs=pl.BlockSpec((tr, c), lambda i: (i, 0)),
        out_shape=jax.ShapeDtypeStruct((rows, c), BF16),
        scratch_shapes=[pltpu.VMEM((tr + 2 * CONV_HALO, c), F32),
                        pltpu.VMEM((tr, c), F32)],
        compiler_params=_cparams("parallel"),
        name="conformer_conv",
    )(u, u, u, dw_w, dw_b.reshape(1, c), ln_g.reshape(1, c), ln_b.reshape(1, c))


R_EID, R_WT, R_RANK = 0, 2, 4


def _part_of_tile(i, bounds):
    return sum((i >= b).astype(jnp.int32) for b in bounds[1:]) if len(bounds) > 1 else 0 * i


def _router_kernel(n_groups, per_group, tm, bounds, h_ref, w_ref, info_ref, cnt_ref, carry_ref):
    i = pl.program_id(0)
    starts_part = functools.reduce(jnp.logical_or, [i == b for b in bounds])

    @pl.when(starts_part)
    def _():
        carry_ref[...] = jnp.zeros(carry_ref.shape, F32)

    lg = jnp.dot(h_ref[...], w_ref[...], preferred_element_type=F32)
    lane = lax.broadcasted_iota(jnp.int32, lg.shape, 1).astype(F32)
    is_g = lane < n_groups
    gl = jnp.where(is_g, lg, NEG)
    mg = jnp.max(gl, axis=-1, keepdims=True)
    gsel = jnp.min(jnp.where(gl == mg, lane, float(LANES)), axis=-1, keepdims=True)
    gsum = jnp.sum(jnp.where(is_g, jnp.exp(gl - mg), 0.0), axis=-1, keepdims=True)
    lo = n_groups + gsel * per_group
    el = jnp.where((lane >= lo) & (lane < lo + per_group), lg, NEG)
    v1 = jnp.max(el, axis=-1, keepdims=True)
    i1 = jnp.min(jnp.where(el == v1, lane, float(LANES)), axis=-1, keepdims=True)
    el2 = jnp.where(lane == i1, NEG, el)
    v2 = jnp.max(el2, axis=-1, keepdims=True)
    i2 = jnp.min(jnp.where(el2 == v2, lane, float(LANES)), axis=-1, keepdims=True)
    e2 = jnp.exp(v2 - v1)
    w1 = 1.0 / ((1.0 + e2) * gsum)
    w2 = e2 * w1
    hit1 = lane == i1
    hit2 = lane == i2
    cnt = jnp.where(hit1 | hit2, 1.0, 0.0)
    tri = (lax.broadcasted_iota(jnp.int32, (tm, tm), 1) < lax.broadcasted_iota(jnp.int32, (tm, tm), 0))
    before = jnp.dot(jnp.where(tri, 1.0, 0.0).astype(BF16), cnt.astype(BF16), preferred_element_type=F32)
    tot = carry_ref[0:1, :] + before
    r1 = jnp.sum(jnp.where(hit1, tot, 0.0), axis=-1, keepdims=True)
    r2 = jnp.sum(jnp.where(hit2, tot, 0.0), axis=-1, keepdims=True)
    carry_ref[...] = carry_ref[...] + jnp.sum(cnt, axis=0, keepdims=True)
    info = jnp.zeros(lg.shape, F32)
    for col, val in ((R_EID, i1 - n_groups), (R_EID + 1, i2 - n_groups), (R_WT, w1), (R_WT + 1, w2),
                     (R_RANK, r1), (R_RANK + 1, r2)):
        info = jnp.where(lane == col, val, info)
    info_ref[...] = info
    cnt_ref[...] = carry_ref[...]


def _router(h, w_r, n_groups, per_group, rows, part_starts):
    d = h.shape[1]
    tm = MOE_TILE
    assert all(s % tm == 0 for s in part_starts) and rows % tm == 0
    bounds = tuple(s // tm for s in part_starts)
    return pl.pallas_call(
        functools.partial(_router_kernel, n_groups, per_group, tm, bounds),
        grid=(rows // tm,),
        in_specs=[pl.BlockSpec((tm, d), lambda i: (i, 0)),
                  pl.BlockSpec((d, LANES), lambda i: (0, 0))],
        out_specs=[pl.BlockSpec((tm, LANES), lambda i: (i, 0)),
                   pl.BlockSpec((8, LANES), lambda i: (_part_of_tile(i, bounds), 0))],
        out_shape=[jax.ShapeDtypeStruct((rows, LANES), F32),
                   jax.ShapeDtypeStruct((8 * len(bounds), LANES), F32)],
        scratch_shapes=[pltpu.VMEM((8, LANES), F32)],
        compiler_params=_cparams("arbitrary"),
        name="moe_router",
    )(h, w_r)


def _cast_on_expert_change(te_ref, pairs):
    i = pl.program_id(0)
    prev = te_ref[jnp.maximum(i - 1, 0)]

    @pl.when(jnp.logical_or(i == 0, te_ref[i] != prev))
    def _():
        for w_ref, wb_ref in pairs:
            wb_ref[...] = w_ref[0, 0].astype(wb_ref.dtype)


def _expert_up_kernel(te_ref, tv_ref, x_ref, wg_ref, wu_ref, o_ref, wgb_ref, wub_ref):
    i = pl.program_id(0)
    _cast_on_expert_change(te_ref, ((wg_ref, wgb_ref), (wu_ref, wub_ref)))

    @pl.when(tv_ref[i] > 0)
    def _():
        x = x_ref[...]
        hg = jnp.dot(x, wgb_ref[...], preferred_element_type=F32)
        hu = jnp.dot(x, wub_ref[...], preferred_element_type=F32)
        o_ref[...] = ((hg * _sigmoid(hg)) * hu).astype(o_ref.dtype)

    @pl.when(tv_ref[i] == 0)
    def _():
        o_ref[...] = jnp.zeros(o_ref.shape, o_ref.dtype)


def _expert_down_kernel(te_ref, tv_ref, a_ref, wd_ref, o_ref, wdb_ref):
    i = pl.program_id(0)
    _cast_on_expert_change(te_ref, ((wd_ref, wdb_ref),))

    @pl.when(tv_ref[i] > 0)
    def _():
        o_ref[...] = jnp.dot(a_ref[...], wdb_ref[...], preferred_element_type=F32).astype(o_ref.dtype)

    @pl.when(tv_ref[i] == 0)
    def _():
        o_ref[...] = jnp.zeros(o_ref.shape, o_ref.dtype)


def _expert_ffn(xs, tile_e, tile_valid, w_g, w_u, w_d, li, tm):
    p, d = xs.shape
    f = w_g.shape[3]

    def wmap(i, te, tv):
        return (li, te[i], 0, 0)

    def rmap(i, te, tv):
        return (i, 0)

    act = pl.pallas_call(
        _expert_up_kernel,
        grid_spec=pltpu.PrefetchScalarGridSpec(
            num_scalar_prefetch=2,
            grid=(p // tm,),
            in_specs=[pl.BlockSpec((tm, d), rmap),
                      pl.BlockSpec((1, 1, d, f), wmap),
                      pl.BlockSpec((1, 1, d, f), wmap)],
            out_specs=pl.BlockSpec((tm, f), rmap),
            scratch_shapes=[pltpu.VMEM((d, f), BF16), pltpu.VMEM((d, f), BF16)]),
        out_shape=jax.ShapeDtypeStruct((p, f), BF16),
        compiler_params=_cparams("arbitrary"),
        name="moe_up",
    )(tile_e, tile_valid, xs, w_g, w_u)
    return pl.pallas_call(
        _expert_down_kernel,
        grid_spec=pltpu.PrefetchScalarGridSpec(
            num_scalar_prefetch=2,
            grid=(p // tm,),
            in_specs=[pl.BlockSpec((tm, f), rmap),
                      pl.BlockSpec((1, 1, f, d), wmap)],
            out_specs=pl.BlockSpec((tm, d), rmap),
            scratch_shapes=[pltpu.VMEM((f, d), BF16)]),
        out_shape=jax.ShapeDtypeStruct((p, d), BF16),
        compiler_params=_cparams("arbitrary"),
        name="moe_down",
    )(tile_e, tile_valid, act, w_d)


def _combine_kernel(n_lat, tm, tile0, z_ref, ya_ref, yb_ref, info_ref, gl_ref, gc_ref, o_ref):
    is_lat = (pl.program_id(0) + tile0) * tm < n_lat
    gate = jnp.where(is_lat, gl_ref[0], gc_ref[0])
    w1 = info_ref[:, R_WT:R_WT + 1]
    w2 = info_ref[:, R_WT + 1:R_WT + 2]
    y = w1 * ya_ref[...].astype(F32) + w2 * yb_ref[...].astype(F32)
    o_ref[...] = z_ref[...] + gate * y


def _combine(z, y2, info, mods, gate_idx, row0, part_rows, n_lat):
    d = z.shape[1]
    tm = MOE_TILE
    assert row0 % tm == 0 and part_rows % tm == 0 and n_lat % tm == 0
    nt = part_rows // tm
    t0 = row0 // tm
    return pl.pallas_call(
        functools.partial(_combine_kernel, n_lat, tm, t0),
        grid=(nt,),
        in_specs=[pl.BlockSpec((tm, d), lambda i: (i + t0, 0)),
                  pl.BlockSpec((tm, d), lambda i: (i, 0)),
                  pl.BlockSpec((tm, d), lambda i: (i + nt, 0)),
                  pl.BlockSpec((tm, LANES), lambda i: (i + t0, 0)),
                  pl.BlockSpec((1, 1, d), lambda i: (gate_idx, 0, 0)),
                  pl.BlockSpec((1, 1, d), lambda i: (N_MOD + gate_idx, 0, 0))],
        out_specs=pl.BlockSpec((tm, d), lambda i: (i + t0, 0)),
        out_shape=jax.ShapeDtypeStruct(z.shape, F32),
        input_output_aliases={0: 0},
        compiler_params=_cparams("parallel"),
        name="moe_combine",
    )(z, y2, y2, info, mods, mods)


def _rows_of(x, idx):
    return x.at[idx].get(mode="promise_in_bounds", unique_indices=False)


def _moe(h2, z, mods, w_r, w_g, w_u, w_d, li, n_groups, per_group, rows, n_lat):
    n_exp = n_groups * per_group
    tm = MOE_TILE
    first = (rows // tm // MOE_PARTS) * tm
    part_starts = tuple(range(0, first * MOE_PARTS, first))
    info, cnt = _router(h2, w_r, n_groups, per_group, rows, part_starts)
    experts = jnp.arange(n_exp, dtype=jnp.int32)
    for part, row0 in enumerate(part_starts):
        part_rows = (part_starts[part + 1] if part + 1 < len(part_starts) else rows) - row0
        eid = info[row0:row0 + part_rows, R_EID:R_EID + EXPERT_TOPK].astype(jnp.int32)
        rank = info[row0:row0 + part_rows, R_RANK:R_RANK + EXPERT_TOPK].astype(jnp.int32)
        counts = cnt[8 * part, n_groups:n_groups + n_exp].astype(jnp.int32)
        padded = ((counts + tm - 1) // tm) * tm
        ends = jnp.sum(jnp.where(experts[None, :] <= experts[:, None], padded[None, :], 0), axis=1)
        starts = ends - padded
        pos = jnp.sum(jnp.where(eid[:, :, None] == experts[None, None, :], starts[None, None, :], 0),
                      axis=2) + rank
        pos = pos.T.reshape(-1)
        n_pairs = part_rows * EXPERT_TOPK
        n_tiles = (n_pairs + n_exp * (tm - 1)) // tm + 1
        p = n_tiles * tm
        pair_token = row0 + jnp.arange(n_pairs, dtype=jnp.int32) % part_rows
        row_token = jnp.zeros((p,), jnp.int32).at[pos].set(pair_token, mode="promise_in_bounds",
                                                          unique_indices=True)
        tile_start = jnp.arange(n_tiles, dtype=jnp.int32) * tm
        tile_e = jnp.minimum(jnp.sum((ends[None, :] <= tile_start[:, None]).astype(jnp.int32), axis=1),
                             n_exp - 1)
        tile_valid = (tile_start < ends[-1]).astype(jnp.int32)
        xs = _rows_of(h2, row_token)
        ys = _expert_ffn(xs, tile_e, tile_valid, w_g, w_u, w_d, li, tm)
        z = _combine(z, _rows_of(ys, pos), info, mods, 5, row0, part_rows, n_lat)
    return z


def _rope_tables(n_lat, n_ctx):
    t = jnp.arange(n_lat)
    row = (t // GRID_W).astype(F32)
    col = (t % GRID_W).astype(F32)
    inv = ROPE_THETA ** (-jnp.arange(ROPE_PAIRS, dtype=F32) / ROPE_PAIRS)
    ar = row[:, None] * inv
    ac = col[:, None] * inv
    cos = jnp.concatenate([jnp.cos(ar), jnp.cos(ar), jnp.cos(ac), jnp.cos(ac)], axis=1)
    sin = jnp.concatenate([-jnp.sin(ar), jnp.sin(ar), -jnp.sin(ac), jnp.sin(ac)], axis=1)
    cos = jnp.concatenate([cos, jnp.ones((n_ctx, HEAD_DIM), F32)], axis=0)
    sin = jnp.concatenate([sin, jnp.zeros((n_ctx, HEAD_DIM), F32)], axis=0)
    return cos, sin


def _with_ctx(o_lat, o_ctx, n_lat):
    return lax.dynamic_update_slice(o_lat, o_ctx, (n_lat, 0))


def kernel(x, c, ctx, c_ctx, ada_down, ada_up, ada_bias, norm1_g, norm2_g, w_in_ab, w_out_ab, qnorm_a,
           knorm_a, sink_b, w_in_cd, w_out_cd, rpb_c, dw_w, dw_b, ln_d_g, ln_d_b, router_group,
           router_expert, w_gate, w_up, w_down, final_g):
    assert x.shape[0] == 1 and ctx.shape[0] == 1
    n_lat, d = x.shape[1], x.shape[2]
    n_ctx = ctx.shape[1]
    n_all = n_lat + n_ctx
    depth = ada_down.shape[0]
    assert n_lat % n_ctx == 0 and n_ctx % 128 == 0
    n_groups, per_group = router_expert.shape[1], router_expert.shape[3]
    n_mix = d // HEAD_DIM
    a_w = (n_mix // 2) * HEAD_DIM
    akv_w = (n_mix // 8) * HEAD_DIM
    c_w = (n_mix // 2) * HEAD_DIM
    d_ch = d // 2
    qscale = HEAD_DIM ** -0.5 * LOG2E

    z = jnp.concatenate([x[0], ctx[0]], axis=0)
    cond = jnp.zeros((8, d), F32).at[0].set(c[0]).at[1].set(c_ctx)
    mods_all = _modulation_all(cond, ada_down, ada_up, ada_bias)
    tabs = _rope_tables(n_lat, n_ctx)

    for layer in range(depth):
        need_ctx = layer < depth - 1
        rows = n_all if need_ctx else n_lat
        mods = mods_all[layer, :2].reshape(2 * N_MOD, 1, d)
        h = _norm_mod(z, norm1_g[layer], mods, 0, 1, n_all, n_lat, BF16)
        i = layer // 2
        if layer % 2 == 0:
            w = w_in_ab
            sink = sink_b[i] * LOG2E
            col = 0
            qa = _proj_in(h, w, i, col, a_w, n_all, gain=qnorm_a[i], rope_tabs=tabs, scale=qscale)
            col += a_w
            ka = _proj_in(h, w, i, col, akv_w, n_all, gain=knorm_a[i], rope_tabs=tabs)
            col += akv_w
            va = _proj_in(h, w, i, col, akv_w, n_all, ones_ext=True)
            col += akv_w
            qb = _proj_in(h, w, i, col, a_w, n_all, rope_tabs=tabs, scale=qscale)
            col += a_w
            kb = _proj_in(h, w, i, col, akv_w, n_all, rope_tabs=tabs)
            col += akv_w
            vb = _proj_in(h, w, i, col, akv_w, n_all, ones_ext=True)
            o1 = _global_attn(qa, ka, va, n_lat, n_ctx, rows)
            o2 = _window_attn(qb, kb, vb, sink, n_lat, n_ctx, rows)
            if need_ctx:
                o1 = _with_ctx(o1, _ctx_attn(qa, ka, va, None, n_lat, n_ctx), n_lat)
                o2 = _with_ctx(o2, _ctx_attn(qb, kb, vb, sink, n_lat, n_ctx), n_lat)
            w_out = w_out_ab
        else:
            w = w_in_cd
            qc = _proj_in(h, w, i, 0, c_w, n_all, scale=qscale)
            kc = _proj_in(h, w, i, c_w, c_w, n_all)
            vc = _proj_in(h, w, i, 2 * c_w, c_w, n_all, ones_ext=True)
            u = _proj_glu(h, w, i, 3 * c_w, d_ch, n_all)
            bias = _na_bias_table(rpb_c[i], n_lat // GRID_W)
            o1 = _na_attn(qc, kc, vc, bias, n_lat, n_ctx, rows)
            if need_ctx:
                o1 = _with_ctx(o1, _ctx_attn(qc, kc, vc, None, n_lat, n_ctx), n_lat)
            o2 = _conformer_conv(u, dw_w[i], dw_b[i], ln_d_g[i], ln_d_b[i], rows, n_lat)
            w_out = w_out_cd
        z = _proj_out(o1, o2, w_out, i, z, mods, 2, rows, n_lat)
        h2 = _norm_mod(z, norm2_g[layer], mods, 3, 4, rows, n_lat, BF16)
        w_r = jnp.concatenate(
            [router_group[layer], router_expert[layer].transpose(1, 0, 2).reshape(d, n_groups * per_group)],
            axis=1)
        w_r = jnp.pad(w_r, ((0, 0), (0, LANES - w_r.shape[1]))).astype(BF16)
        z = _moe(h2, z, mods, w_r, w_gate, w_up, w_down, layer, n_groups, per_group, rows, n_lat)
    out = _norm_mod(z, final_g, None, 0, 0, n_lat, n_lat, F32)
    return out[None]
```

```python
import functools
import math

import numpy as np
import jax
import jax.numpy as jnp
from jax import lax
from jax.experimental import pallas as pl
from jax.experimental.pallas import tpu as pltpu

HEAD_DIM = 128
GRID_W = 64
ROPE_PAIRS = HEAD_DIM // 4
ROPE_THETA = 10000.0
WINDOW = 128
NA_KH = 8
NA_KW = 16
CONV_W = 31
N_MOD = 6
EXPERT_TOPK = 2
EPS = 1e-6
NEG = -1e30
LOG2E = math.log2(math.e)
LANES = 128
VMEM_LIMIT = 56 * 1024 * 1024
NA_ROWS = 8
CONV_HALO = 16
EXT = 2 * HEAD_DIM
MOE_TILE = 256
MOE_PARTS = 1

BF16 = jnp.bfloat16
F32 = jnp.float32


def _cparams(*sem):
    return pltpu.CompilerParams(dimension_semantics=sem, vmem_limit_bytes=VMEM_LIMIT)


def _row_tile(m, cap):
    best = None
    for t in range(LANES, min(m, cap) + 1, LANES):
        if m % t == 0:
            best = t
    assert best is not None, (m, cap)
    return best


def _sigmoid(x):
    return 1.0 / (1.0 + jnp.exp(-x))


def _mod_down_kernel(cond_ref, down_ref, o_ref):
    cnd = cond_ref[...]
    a = cnd * _sigmoid(cnd)
    o_ref[0] = jnp.dot(a, down_ref[0], preferred_element_type=F32, precision=lax.Precision.HIGHEST)


def _mod_up_kernel(t_ref, up_ref, b_ref, o_ref):
    o_ref[0] = jnp.dot(t_ref[0], up_ref[0], preferred_element_type=F32,
                       precision=lax.Precision.HIGHEST) + b_ref[0]


def _modulation_all(cond, ada_down, ada_up, ada_bias):
    depth, d, r = ada_down.shape
    n = ada_up.shape[2]
    t = pl.pallas_call(
        _mod_down_kernel,
        grid=(depth,),
        in_specs=[pl.BlockSpec((8, d), lambda l: (0, 0)),
                  pl.BlockSpec((1, d, r), lambda l: (l, 0, 0))],
        out_specs=pl.BlockSpec((1, 8, r), lambda l: (l, 0, 0)),
        out_shape=jax.ShapeDtypeStruct((depth, 8, r), F32),
        compiler_params=_cparams("arbitrary"),
        name="mod_down",
    )(cond, ada_down)
    tn = _row_tile(n, 2048)
    return pl.pallas_call(
        _mod_up_kernel,
        grid=(depth, n // tn),
        in_specs=[pl.BlockSpec((1, 8, r), lambda l, j: (l, 0, 0)),
                  pl.BlockSpec((1, r, tn), lambda l, j: (l, 0, j)),
                  pl.BlockSpec((1, 1, tn), lambda l, j: (l, 0, j))],
        out_specs=pl.BlockSpec((1, 8, tn), lambda l, j: (l, 0, j)),
        out_shape=jax.ShapeDtypeStruct((depth, 8, n), F32),
        compiler_params=_cparams("arbitrary", "arbitrary"),
        name="mod_up",
    )(t, ada_up, ada_bias.reshape(depth, 1, n))


def _pack_pairs(x):
    half = x.shape[1] // 2
    lo = lax.bitcast_convert_type(x[:, :half].astype(BF16).astype(F32), jnp.uint32)
    hi = lax.bitcast_convert_type(x[:, half:].astype(BF16).astype(F32), jnp.uint32)
    return lax.shift_right_logical(lo, jnp.uint32(16)) | (hi & jnp.uint32(0xFFFF0000))


def _unpack_pairs(w):
    lo = lax.bitcast_convert_type(lax.shift_left(w, jnp.uint32(16)), F32)
    hi = lax.bitcast_convert_type(w & jnp.uint32(0xFFFF0000), F32)
    return lo, hi


def _norm_mod_kernel(n_lat, tm, modulated, packed, z_ref, g_ref, *rest):
    rest = list(rest)
    p_ref = rest.pop() if packed else None
    if modulated:
        sh_l, sc_l, sh_c, sc_c, o_ref = rest
    else:
        (o_ref,) = rest
    x = z_ref[...]
    r = lax.rsqrt(jnp.mean(x * x, axis=-1, keepdims=True) + EPS)
    if modulated:
        is_lat = pl.program_id(0) * tm < n_lat
        shift = jnp.where(is_lat, sh_l[0], sh_c[0])
        scale = jnp.where(is_lat, sc_l[0], sc_c[0])
        h = (x * r) * (g_ref[...] * (1.0 + scale)) + shift
    else:
        h = x * r * g_ref[...]
    o_ref[...] = h.astype(o_ref.dtype)
    if packed:
        p_ref[...] = _pack_pairs(h)


def _norm_mod(z, g, mods, a, b, rows, n_lat, out_dtype, packed=False):
    d = z.shape[1]
    tm = _row_tile(math.gcd(rows, n_lat), 256)
    modulated = mods is not None
    in_specs = [pl.BlockSpec((tm, d), lambda i: (i, 0)),
                pl.BlockSpec((1, d), lambda i: (0, 0))]
    args = [z, g.reshape(1, d)]
    if modulated:
        for idx in (a, b, N_MOD + a, N_MOD + b):
            in_specs.append(pl.BlockSpec((1, 1, d), lambda i, idx=idx: (idx, 0, 0)))
            args.append(mods)
    out_specs = pl.BlockSpec((tm, d), lambda i: (i, 0))
    out_shape = jax.ShapeDtypeStruct((rows, d), out_dtype)
    if packed:
        out_specs = [out_specs, pl.BlockSpec((tm, d // 2), lambda i: (i, 0))]
        out_shape = [out_shape, jax.ShapeDtypeStruct((rows, d // 2), jnp.uint32)]
    return pl.pallas_call(
        functools.partial(_norm_mod_kernel, n_lat, tm, modulated, packed),
        grid=(rows // tm,),
        in_specs=in_specs,
        out_specs=out_specs,
        out_shape=out_shape,
        compiler_params=_cparams("parallel"),
        name="norm_mod",
    )(*args)


def _rope_swap(x):
    lane = lax.broadcasted_iota(jnp.int32, x.shape, 1)
    first_half = (lane % (2 * ROPE_PAIRS)) < ROPE_PAIRS
    return jnp.where(first_half, pltpu.roll(x, HEAD_DIM - ROPE_PAIRS, 1), pltpu.roll(x, ROPE_PAIRS, 1))


def _cast_weight_tile(w_ref, wb_ref):
    @pl.when(pl.program_id(1) == 0)
    def _():
        wb_ref[...] = w_ref[0].astype(wb_ref.dtype)


def _proj_in_kernel(norm, rope, scale, ones_ext, heads, a_ref, w_ref, *rest):
    rest = list(rest)
    g_ref = rest.pop(0) if norm else None
    if rope:
        c_ref = rest.pop(0)
        s_ref = rest.pop(0)
    o_ref, wb_ref = rest
    _cast_weight_tile(w_ref, wb_ref)
    acc = jnp.dot(a_ref[...], wb_ref[...], preferred_element_type=F32)
    if not (norm or rope or ones_ext):
        if scale != 1.0:
            acc = acc * scale
        o_ref[...] = acc.astype(o_ref.dtype)
        return
    ow = EXT if ones_ext else HEAD_DIM
    for hh in range(heads):
        x = acc[:, hh * HEAD_DIM:(hh + 1) * HEAD_DIM]
        if norm:
            r = lax.rsqrt(jnp.mean(x * x, axis=-1, keepdims=True) + EPS)
            x = x * r * g_ref[...]
        if rope:
            x = x * c_ref[...] + _rope_swap(x) * s_ref[...]
        if scale != 1.0:
            x = x * scale
        o_ref[:, hh * ow:hh * ow + HEAD_DIM] = x.astype(o_ref.dtype)
        if ones_ext:
            o_ref[:, hh * ow + HEAD_DIM:(hh + 1) * ow] = jnp.ones(x.shape, o_ref.dtype)


def _proj_in(h, w, li, col0, width, rows, *, gain=None, rope_tabs=None, scale=1.0, ones_ext=False):
    k = h.shape[1]
    tm = _row_tile(rows, 1280)
    tn = math.gcd(math.gcd(width, 512), col0) if col0 else math.gcd(width, 512)
    norm = gain is not None
    rope = rope_tabs is not None
    in_specs = [pl.BlockSpec((tm, k), lambda j, i: (i, 0)),
                pl.BlockSpec((1, k, tn), lambda j, i: (li, 0, col0 // tn + j))]
    args = [h, w]
    if norm:
        in_specs.append(pl.BlockSpec((1, HEAD_DIM), lambda j, i: (0, 0)))
        args.append(gain.reshape(1, HEAD_DIM))
    if rope:
        for tab in rope_tabs:
            in_specs.append(pl.BlockSpec((tm, HEAD_DIM), lambda j, i: (i, 0)))
            args.append(tab)
    mult = 2 if ones_ext else 1
    return pl.pallas_call(
        functools.partial(_proj_in_kernel, norm, rope, scale, ones_ext, tn // HEAD_DIM),
        grid=(width // tn, rows // tm),
        in_specs=in_specs,
        out_specs=pl.BlockSpec((tm, mult * tn), lambda j, i: (i, j)),
        out_shape=jax.ShapeDtypeStruct((rows, mult * width), BF16),
        scratch_shapes=[pltpu.VMEM((k, tn), BF16)],
        compiler_params=_cparams("parallel", "arbitrary"),
        name="proj_in",
    )(*args)


def _proj_glu_kernel(a_ref, wa_ref, wg_ref, o_ref, wab_ref, wgb_ref):
    _cast_weight_tile(wa_ref, wab_ref)
    _cast_weight_tile(wg_ref, wgb_ref)
    a = a_ref[...]
    va = jnp.dot(a, wab_ref[...], preferred_element_type=F32)
    vg = jnp.dot(a, wgb_ref[...], preferred_element_type=F32)
    o_ref[...] = (va * _sigmoid(vg)).astype(o_ref.dtype)


def _proj_glu(h, w, li, col0, width, rows):
    k = h.shape[1]
    tm = _row_tile(rows, 1280)
    tn = math.gcd(math.gcd(width, 256), col0)
    return pl.pallas_call(
        _proj_glu_kernel,
        grid=(width // tn, rows // tm),
        in_specs=[pl.BlockSpec((tm, k), lambda j, i: (i, 0)),
                  pl.BlockSpec((1, k, tn), lambda j, i: (li, 0, col0 // tn + j)),
                  pl.BlockSpec((1, k, tn), lambda j, i: (li, 0, (col0 + width) // tn + j))],
        out_specs=pl.BlockSpec((tm, tn), lambda j, i: (i, j)),
        out_shape=jax.ShapeDtypeStruct((rows, width), F32),
        scratch_shapes=[pltpu.VMEM((k, tn), BF16), pltpu.VMEM((k, tn), BF16)],
        compiler_params=_cparams("parallel", "arbitrary"),
        name="proj_glu",
    )(h, w, w)


def _proj_out_kernel(n_lat, tm, a1_ref, a2_ref, w1_ref, w2_ref, z_ref, gl_ref, gc_ref, o_ref,
                     w1b_ref, w2b_ref):
    _cast_weight_tile(w1_ref, w1b_ref)
    _cast_weight_tile(w2_ref, w2b_ref)
    acc = jnp.dot(a1_ref[...], w1b_ref[...], preferred_element_type=F32)
    acc = acc + jnp.dot(a2_ref[...], w2b_ref[...], preferred_element_type=F32)
    rows = pl.program_id(1) * tm + lax.broadcasted_iota(jnp.int32, (tm, 1), 0)
    gate = jnp.where(rows < n_lat, gl_ref[0], gc_ref[0])
    o_ref[...] = z_ref[...] + gate * acc


def _proj_out(a1, a2, w, li, z, mods, gate_idx, rows, n_lat):
    k1, k2 = a1.shape[1], a2.shape[1]
    assert k1 == k2
    n = w.shape[2]
    tm = _row_tile(rows, 1280)
    tn = math.gcd(n, 512)
    return pl.pallas_call(
        functools.partial(_proj_out_kernel, n_lat, tm),
        grid=(n // tn, rows // tm),
        in_specs=[pl.BlockSpec((tm, k1), lambda j, i: (i, 0)),
                  pl.BlockSpec((tm, k2), lambda j, i: (i, 0)),
                  pl.BlockSpec((1, k1, tn), lambda j, i: (li, 0, j)),
                  pl.BlockSpec((1, k2, tn), lambda j, i: (li, 1, j)),
                  pl.BlockSpec((tm, tn), lambda j, i: (i, j)),
                  pl.BlockSpec((1, 1, tn), lambda j, i: (gate_idx, 0, j)),
                  pl.BlockSpec((1, 1, tn), lambda j, i: (N_MOD + gate_idx, 0, j))],
        out_specs=pl.BlockSpec((tm, tn), lambda j, i: (i, j)),
        out_shape=jax.ShapeDtypeStruct((rows, n), F32),
        scratch_shapes=[pltpu.VMEM((k1, tn), BF16), pltpu.VMEM((k2, tn), BF16)],
        compiler_params=_cparams("parallel", "arbitrary"),
        name="proj_out",
    )(a1, a2, w, w, z, mods, mods)


def _nt_dot(a, b):
    return lax.dot_general(a, b, (((1,), (1,)), ((), ())), preferred_element_type=F32)


def _softmax_pv(s, v_ext, m_ref, acc_ref):
    m_prev = m_ref[...]
    m_new = jnp.maximum(m_prev, jnp.max(s, axis=1, keepdims=True))
    alpha = jnp.exp2(m_prev - m_new)
    p = jnp.exp2(s - jnp.tile(m_new, (1, s.shape[1] // LANES)))
    pv = jnp.dot(p.astype(v_ext.dtype), v_ext, preferred_element_type=F32)
    acc_ref[...] = jnp.tile(alpha, (1, EXT // LANES)) * acc_ref[...] + pv
    m_ref[...] = m_new


def _softmax_chunk(q, k, v_ext, m_ref, acc_ref, mask=None):
    s = _nt_dot(q, k)
    if mask is not None:
        s = jnp.where(mask, s, NEG)
    _softmax_pv(s, v_ext, m_ref, acc_ref)


def _stack_heads(groups, tq, q_ref, qs_ref):
    for g in range(groups):
        qs_ref[g * tq:(g + 1) * tq, :] = q_ref[:, g * HEAD_DIM:(g + 1) * HEAD_DIM]


def _write_heads(groups, tq, acc_ref, o_ref):
    for g in range(groups):
        a = acc_ref[g * tq:(g + 1) * tq, :]
        o_ref[:, g * HEAD_DIM:(g + 1) * HEAD_DIM] = (a[:, :HEAD_DIM] / a[:, HEAD_DIM:]).astype(o_ref.dtype)


def _global_attn_kernel(groups, tq, kc, n_chunks, q_ref, k_ref, v_ref, kc_ref, vc_ref, o_ref,
                        qs_ref, m_ref, acc_ref, s0_ref, s1_ref):
    _stack_heads(groups, tq, q_ref, qs_ref)
    m_ref[...] = jnp.full(m_ref.shape, NEG, F32)
    acc_ref[...] = jnp.zeros(acc_ref.shape, F32)
    _softmax_chunk(qs_ref[...], kc_ref[...], vc_ref[...], m_ref, acc_ref)

    def logits(c):
        return _nt_dot(qs_ref[...], k_ref[pl.ds(pl.multiple_of(c * kc, kc), kc), :])

    def update(s_ref, c):
        _softmax_pv(s_ref[...], v_ref[pl.ds(pl.multiple_of(c * kc, kc), kc), :], m_ref, acc_ref)

    s0_ref[...] = logits(0)

    def body(t, carry):
        c = 2 * t
        s1_ref[...] = logits(c + 1)
        update(s0_ref, c)
        s0_ref[...] = logits(c + 2)
        update(s1_ref, c + 1)
        return carry

    lax.fori_loop(0, n_chunks // 2 - 1, body, 0)
    s1_ref[...] = logits(n_chunks - 1)
    update(s0_ref, n_chunks - 2)
    update(s1_ref, n_chunks - 1)
    _write_heads(groups, tq, acc_ref, o_ref)


def _global_attn(q, k, v_ext, n_lat, n_ctx, out_rows):
    hq = q.shape[1] // HEAD_DIM
    hkv = k.shape[1] // HEAD_DIM
    groups = hq // hkv
    tq = _row_tile(n_lat, 256)
    kc = _row_tile(n_lat, 512)
    n_chunks = n_lat // kc
    assert n_chunks >= 2 and n_chunks % 2 == 0
    gw = groups * HEAD_DIM
    ctx_blk = n_lat // n_ctx
    rows = groups * tq
    return pl.pallas_call(
        functools.partial(_global_attn_kernel, groups, tq, kc, n_chunks),
        grid=(hkv, n_lat // tq),
        in_specs=[pl.BlockSpec((tq, gw), lambda h, i: (i, h)),
                  pl.BlockSpec((n_lat, HEAD_DIM), lambda h, i: (0, h)),
                  pl.BlockSpec((n_lat, EXT), lambda h, i: (0, h)),
                  pl.BlockSpec((n_ctx, HEAD_DIM), lambda h, i: (ctx_blk, h)),
                  pl.BlockSpec((n_ctx, EXT), lambda h, i: (ctx_blk, h))],
        out_specs=pl.BlockSpec((tq, gw), lambda h, i: (i, h)),
        out_shape=jax.ShapeDtypeStruct((out_rows, q.shape[1]), BF16),
        scratch_shapes=[pltpu.VMEM((rows, HEAD_DIM), BF16),
                        pltpu.VMEM((rows, LANES), F32),
                        pltpu.VMEM((rows, EXT), F32),
                        pltpu.VMEM((rows, kc), F32),
                        pltpu.VMEM((rows, kc), F32)],
        compiler_params=_cparams("parallel", "arbitrary"),
        name="global_attn",
    )(q, k, v_ext, k, v_ext)


def _window_attn_kernel(groups, tq, span, n_lat, sink_ref, q_ref, k_ref, v_ref, kc_ref, vc_ref, o_ref,
                        qs_ref, m_ref, acc_ref):
    h = pl.program_id(0)
    i = pl.program_id(1)
    rows = groups * tq
    _stack_heads(groups, tq, q_ref, qs_ref)
    for g in range(groups):
        m_ref[g * tq:(g + 1) * tq, :] = jnp.full((tq, LANES), sink_ref[h * groups + g], F32)
    lane = lax.broadcasted_iota(jnp.int32, (rows, EXT), 1)
    acc_ref[...] = jnp.where(lane >= HEAD_DIM, 1.0, 0.0).astype(F32)
    _softmax_chunk(qs_ref[...], kc_ref[...], vc_ref[...], m_ref, acc_ref)
    start = pl.multiple_of(jnp.clip(i * tq - WINDOW, 0, n_lat - span), WINDOW)
    qpos = i * tq + lax.broadcasted_iota(jnp.int32, (tq, span), 0)
    kpos = start + lax.broadcasted_iota(jnp.int32, (tq, span), 1)
    band = jnp.where(jnp.abs(qpos - kpos) <= WINDOW, 0.0, NEG)
    s = _nt_dot(qs_ref[...], k_ref[pl.ds(start, span), :])
    s = (s.reshape(groups, tq, span) + band[None]).reshape(rows, span)
    _softmax_pv(s, v_ref[pl.ds(start, span), :], m_ref, acc_ref)
    _write_heads(groups, tq, acc_ref, o_ref)


def _window_attn(q, k, v_ext, sink, n_lat, n_ctx, out_rows):
    hq = q.shape[1] // HEAD_DIM
    hkv = k.shape[1] // HEAD_DIM
    groups = hq // hkv
    tq = _row_tile(n_lat, 256)
    span = tq + 2 * WINDOW
    assert tq % WINDOW == 0 and n_lat >= span
    gw = groups * HEAD_DIM
    ctx_blk = n_lat // n_ctx
    rows = groups * tq
    return pl.pallas_call(
        functools.partial(_window_attn_kernel, groups, tq, span, n_lat),
        grid=(hkv, n_lat // tq),
        in_specs=[pl.BlockSpec(memory_space=pltpu.SMEM),
                  pl.BlockSpec((tq, gw), lambda h, i: (i, h)),
                  pl.BlockSpec((n_lat, HEAD_DIM), lambda h, i: (0, h)),
                  pl.BlockSpec((n_lat, EXT), lambda h, i: (0, h)),
                  pl.BlockSpec((n_ctx, HEAD_DIM), lambda h, i: (ctx_blk, h)),
                  pl.BlockSpec((n_ctx, EXT), lambda h, i: (ctx_blk, h))],
        out_specs=pl.BlockSpec((tq, gw), lambda h, i: (i, h)),
        out_shape=jax.ShapeDtypeStruct((out_rows, q.shape[1]), BF16),
        scratch_shapes=[pltpu.VMEM((rows, HEAD_DIM), BF16),
                        pltpu.VMEM((rows, LANES), F32),
                        pltpu.VMEM((rows, EXT), F32)],
        compiler_params=_cparams("parallel", "arbitrary"),
        name="window_attn",
    )(sink, q, k, v_ext, k, v_ext)


def _ctx_attn_kernel(groups, has_sink, *refs):
    if has_sink:
        sink_ref, q_ref, k_ref, v_ref, o_ref = refs
    else:
        q_ref, k_ref, v_ref, o_ref = refs
    h = pl.program_id(0)
    for g in range(groups):
        q = q_ref[:, g * HEAD_DIM:(g + 1) * HEAD_DIM]
        s = _nt_dot(q, k_ref[...])
        m = jnp.max(s, axis=-1, keepdims=True)
        if has_sink:
            snk = sink_ref[h * groups + g]
            m = jnp.maximum(m, snk)
        p = jnp.exp2(s - m)
        o = jnp.dot(p.astype(v_ref.dtype), v_ref[...], preferred_element_type=F32)
        l = o[:, HEAD_DIM:]
        if has_sink:
            l = l + jnp.exp2(snk - m)
        o_ref[:, g * HEAD_DIM:(g + 1) * HEAD_DIM] = (o[:, :HEAD_DIM] / l).astype(o_ref.dtype)


def _ctx_attn(q, k, v_ext, sink, n_lat, n_ctx):
    hq = q.shape[1] // HEAD_DIM
    hkv = k.shape[1] // HEAD_DIM
    groups = hq // hkv
    gw = groups * HEAD_DIM
    ctx_blk = n_lat // n_ctx
    has_sink = sink is not None
    in_specs = [pl.BlockSpec((n_ctx, gw), lambda h: (ctx_blk, h)),
                pl.BlockSpec((n_ctx, HEAD_DIM), lambda h: (ctx_blk, h)),
                pl.BlockSpec((n_ctx, EXT), lambda h: (ctx_blk, h))]
    args = [q, k, v_ext]
    if has_sink:
        in_specs.insert(0, pl.BlockSpec(memory_space=pltpu.SMEM))
        args.insert(0, sink)
    return pl.pallas_call(
        functools.partial(_ctx_attn_kernel, groups, has_sink),
        grid=(hkv,),
        in_specs=in_specs,
        out_specs=pl.BlockSpec((n_ctx, gw), lambda h: (0, h)),
        out_shape=jax.ShapeDtypeStruct((n_ctx, q.shape[1]), BF16),
        compiler_params=_cparams("parallel"),
        name="ctx_attn",
    )(*args)


def _na_bias_table(rpb, n_rows):
    kh = min(NA_KH, n_rows)
    assert kh == NA_KH and NA_ROWS == NA_KH and n_rows % NA_ROWS == 0 and n_rows // NA_ROWS >= 2
    half = NA_ROWS // 2
    qr = np.arange(NA_ROWS)
    krel = np.where(qr[:, None] < half, np.arange(-half, 0)[None, :], NA_ROWS + np.arange(half)[None, :])
    krel = np.concatenate([krel, np.broadcast_to(np.arange(NA_ROWS), (NA_ROWS, NA_ROWS))], axis=1)
    nk = krel.shape[1]
    ri = np.zeros((3, NA_ROWS, nk), np.int32)
    rvalid = np.zeros((3, NA_ROWS, nk), bool)
    big = 4 * NA_ROWS
    for var, (r0, rows) in enumerate(((0, big), (NA_ROWS, big), (n_rows - NA_ROWS, n_rows))):
        r = r0 + qr[:, None]
        key = r0 + krel
        rs = np.clip(r - kh // 2, 0, rows - kh)
        rvalid[var] = (key >= rs) & (key < rs + kh)
        ri[var] = np.clip(key - r + NA_KH - 1, 0, 2 * NA_KH - 2)
    qc = np.arange(GRID_W)
    cs = np.clip(qc - NA_KW // 2, 0, GRID_W - NA_KW)
    kc = np.arange(GRID_W)
    cvalid = (kc[None, :] >= cs[:, None]) & (kc[None, :] < cs[:, None] + NA_KW)
    ci = np.clip(kc[None, :] - qc[:, None] + NA_KW - 1, 0, 2 * NA_KW - 2)
    t = jnp.take(rpb * LOG2E, jnp.asarray(ci), axis=2)
    t = t[:, jnp.asarray(ri)]
    valid = rvalid[:, :, :, None, None] & cvalid[None, None, None, :, :]
    t = jnp.where(jnp.asarray(valid)[None], t, NEG)
    t = t.transpose(0, 1, 2, 4, 3, 5)
    return t.reshape(rpb.shape[0], 3, NA_ROWS * GRID_W, nk * GRID_W)


def _na_kernel(nq, q_ref, kp_ref, k0_ref, kn_ref, vp_ref, v0_ref, vn_ref, kc_ref, vc_ref, b_ref, o_ref):
    half = nq // 2
    q = q_ref[...]
    s_nb = jnp.concatenate([_nt_dot(q[:half], kp_ref[...]), _nt_dot(q[half:], kn_ref[...])], axis=0)
    s_nb = s_nb + b_ref[0, 0, :, :half]
    s_own = _nt_dot(q, k0_ref[...]) + b_ref[0, 0, :, half:]
    s_ctx = _nt_dot(q, kc_ref[...])
    m = jnp.maximum(jnp.maximum(s_nb.max(axis=-1, keepdims=True), s_own.max(axis=-1, keepdims=True)),
                    s_ctx.max(axis=-1, keepdims=True))
    p_nb = jnp.exp2(s_nb - m).astype(vp_ref.dtype)
    p_own = jnp.exp2(s_own - m).astype(v0_ref.dtype)
    p_ctx = jnp.exp2(s_ctx - m).astype(vc_ref.dtype)
    o = jnp.concatenate([jnp.dot(p_nb[:half], vp_ref[...], preferred_element_type=F32),
                         jnp.dot(p_nb[half:], vn_ref[...], preferred_element_type=F32)], axis=0)
    o = o + jnp.dot(p_own, v0_ref[...], preferred_element_type=F32)
    o = o + jnp.dot(p_ctx, vc_ref[...], preferred_element_type=F32)
    o_ref[...] = (o[:, :HEAD_DIM] / o[:, HEAD_DIM:]).astype(o_ref.dtype)


def _na_attn(q, k, v_ext, bias, n_lat, n_ctx, out_rows):
    heads = q.shape[1] // HEAD_DIM
    nq = NA_ROWS * GRID_W
    half = nq // 2
    nb = n_lat // nq
    ctx_blk = n_lat // n_ctx

    def prev_map(h, b):
        return (jnp.maximum(2 * b - 1, 0), h)

    def next_map(h, b):
        return (jnp.minimum(2 * b + 2, 2 * nb - 1), h)

    def own_map(h, b):
        return (b, h)

    def ctx_map(h, b):
        return (ctx_blk, h)

    def bias_map(h, b):
        return (h, jnp.where(b == 0, 0, jnp.where(b == nb - 1, 2, 1)), 0, 0)

    return pl.pallas_call(
        functools.partial(_na_kernel, nq),
        grid=(heads, nb),
        in_specs=[pl.BlockSpec((nq, HEAD_DIM), own_map),
                  pl.BlockSpec((half, HEAD_DIM), prev_map), pl.BlockSpec((nq, HEAD_DIM), own_map),
                  pl.BlockSpec((half, HEAD_DIM), next_map),
                  pl.BlockSpec((half, EXT), prev_map), pl.BlockSpec((nq, EXT), own_map),
                  pl.BlockSpec((half, EXT), next_map),
                  pl.BlockSpec((n_ctx, HEAD_DIM), ctx_map), pl.BlockSpec((n_ctx, EXT), ctx_map),
                  pl.BlockSpec((1, 1, nq, half + nq), bias_map)],
        out_specs=pl.BlockSpec((nq, HEAD_DIM), own_map),
        out_shape=jax.ShapeDtypeStruct((out_rows, q.shape[1]), BF16),
        compiler_params=_cparams("parallel", "arbitrary"),
        name="na_attn",
    )(q, k, k, k, v_ext, v_ext, v_ext, k, v_ext, bias)


def _conv_kernel(tr, n_lat_tiles, n_tiles, ch_chunk, prev_ref, cur_ref, next_ref, w_ref, b_ref,
                 g_ref, beta_ref, o_ref, win_ref, y_ref):
    i = pl.program_id(0)
    c = cur_ref.shape[1]
    prev_ok = jnp.logical_and(i != 0, i != n_lat_tiles)
    next_ok = jnp.logical_and(i != n_lat_tiles - 1, i != n_tiles - 1)
    win_ref[0:CONV_HALO, :] = jnp.where(prev_ok, prev_ref[...], 0.0)
    win_ref[CONV_HALO:CONV_HALO + tr, :] = cur_ref[...]
    win_ref[CONV_HALO + tr:, :] = jnp.where(next_ok, next_ref[...], 0.0)
    off = CONV_HALO - CONV_W // 2
    rc = 64

    def col_body(cc, carry):
        c0 = pl.multiple_of(cc * ch_chunk, ch_chunk)
        for r0 in range(0, tr, rc):
            acc = jnp.zeros((rc, ch_chunk), F32)
            for t in range(CONV_W):
                acc = acc + win_ref[r0 + off + t:r0 + off + t + rc, pl.ds(c0, ch_chunk)] * \
                    w_ref[t:t + 1, pl.ds(c0, ch_chunk)]
            y_ref[r0:r0 + rc, pl.ds(c0, ch_chunk)] = acc + b_ref[:, pl.ds(c0, ch_chunk)]
        return carry

    lax.fori_loop(0, c // ch_chunk, col_body, 0)
    y = y_ref[...]
    mu = jnp.mean(y, axis=-1, keepdims=True)
    var = jnp.mean(jnp.square(y - mu), axis=-1, keepdims=True)
    yn = (y - mu) * lax.rsqrt(var + EPS) * g_ref[...] + beta_ref[...]
    o_ref[...] = (yn * _sigmoid(yn)).astype(o_ref.dtype)


def _conformer_conv(u, dw_w, dw_b, ln_g, ln_b, rows, n_lat):
    c = u.shape[1]
    tr = 256
    assert n_lat % tr == 0 and rows % tr == 0
    hb = tr // CONV_HALO
    n_tiles = rows // tr
    last_halo = rows // CONV_HALO - 1
    ch_chunk = math.gcd(c, 256)
    return pl.pallas_call(
        functools.partial(_conv_kernel, tr, n_lat // tr, n_tiles, ch_chunk),
        grid=(n_tiles,),
        in_specs=[pl.BlockSpec((CONV_HALO, c), lambda i: (jnp.maximum(i * hb - 1, 0), 0)),
                  pl.BlockSpec((tr, c), lambda i: (i, 0)),
                  pl.BlockSpec((CONV_HALO, c), lambda i: (jnp.minimum((i + 1) * hb, last_halo), 0)),
                  pl.BlockSpec((CONV_W, c), lambda i: (0, 0)),
                  pl.BlockSpec((1, c), lambda i: (0, 0)),
                  pl.BlockSpec((1, c), lambda i: (0, 0)),
                  pl.BlockSpec((1, c), lambda i: (0, 0))],
        out_specs=pl.BlockSpec((tr, c), lambda i: (i, 0)),
        out_shape=jax.ShapeDtypeStruct((rows, c), BF16),
        scratch_shapes=[pltpu.VMEM((tr + 2 * CONV_HALO, c), F32),
                        pltpu.VMEM((tr, c), F32)],
        compiler_params=_cparams("parallel"),
        name="conformer_conv",
    )(u, u, u, dw_w, dw_b.reshape(1, c), ln_g.reshape(1, c), ln_b.reshape(1, c))


R_EID, R_WT, R_RANK = 0, 2, 4


def _part_of_tile(i, bounds):
    return sum((i >= b).astype(jnp.int32) for b in bounds[1:]) if len(bounds) > 1 else 0 * i


def _router_kernel(n_groups, per_group, tm, bounds, h_ref, w_ref, info_ref, cnt_ref, carry_ref):
    i = pl.program_id(0)
    starts_part = functools.reduce(jnp.logical_or, [i == b for b in bounds])

    @pl.when(starts_part)
    def _():
        carry_ref[...] = jnp.zeros(carry_ref.shape, F32)

    lg = jnp.dot(h_ref[...], w_ref[...], preferred_element_type=F32)
    lane = lax.broadcasted_iota(jnp.int32, lg.shape, 1).astype(F32)
    is_g = lane < n_groups
    gl = jnp.where(is_g, lg, NEG)
    mg = jnp.max(gl, axis=-1, keepdims=True)
    gsel = jnp.min(jnp.where(gl == mg, lane, float(LANES)), axis=-1, keepdims=True)
    gsum = jnp.sum(jnp.where(is_g, jnp.exp(gl - mg), 0.0), axis=-1, keepdims=True)
    lo = n_groups + gsel * per_group
    el = jnp.where((lane >= lo) & (lane < lo + per_group), lg, NEG)
    v1 = jnp.max(el, axis=-1, keepdims=True)
    i1 = jnp.min(jnp.where(el == v1, lane, float(LANES)), axis=-1, keepdims=True)
    el2 = jnp.where(lane == i1, NEG, el)
    v2 = jnp.max(el2, axis=-1, keepdims=True)
    i2 = jnp.min(jnp.where(el2 == v2, lane, float(LANES)), axis=-1, keepdims=True)
    e2 = jnp.exp(v2 - v1)
    w1 = 1.0 / ((1.0 + e2) * gsum)
    w2 = e2 * w1
    hit1 = lane == i1
    hit2 = lane == i2
    cnt = jnp.where(hit1 | hit2, 1.0, 0.0)
    tri = (lax.broadcasted_iota(jnp.int32, (tm, tm), 1) < lax.broadcasted_iota(jnp.int32, (tm, tm), 0))
    before = jnp.dot(jnp.where(tri, 1.0, 0.0).astype(BF16), cnt.astype(BF16), preferred_element_type=F32)
    tot = carry_ref[0:1, :] + before
    r1 = jnp.sum(jnp.where(hit1, tot, 0.0), axis=-1, keepdims=True)
    r2 = jnp.sum(jnp.where(hit2, tot, 0.0), axis=-1, keepdims=True)
    carry_ref[...] = carry_ref[...] + jnp.sum(cnt, axis=0, keepdims=True)
    info = jnp.zeros(lg.shape, F32)
    for col, val in ((R_EID, i1 - n_groups), (R_EID + 1, i2 - n_groups), (R_WT, w1), (R_WT + 1, w2),
                     (R_RANK, r1), (R_RANK + 1, r2)):
        info = jnp.where(lane == col, val, info)
    info_ref[...] = info
    cnt_ref[...] = carry_ref[...]


def _router(h, w_r, n_groups, per_group, rows, part_starts):
    d = h.shape[1]
    tm = MOE_TILE
    assert all(s % tm == 0 for s in part_starts) and rows % tm == 0
    bounds = tuple(s // tm for s in part_starts)
    return pl.pallas_call(
        functools.partial(_router_kernel, n_groups, per_group, tm, bounds),
        grid=(rows // tm,),
        in_specs=[pl.BlockSpec((tm, d), lambda i: (i, 0)),
                  pl.BlockSpec((d, LANES), lambda i: (0, 0))],
        out_specs=[pl.BlockSpec((tm, LANES), lambda i: (i, 0)),
                   pl.BlockSpec((8, LANES), lambda i: (_part_of_tile(i, bounds), 0))],
        out_shape=[jax.ShapeDtypeStruct((rows, LANES), F32),
                   jax.ShapeDtypeStruct((8 * len(bounds), LANES), F32)],
        scratch_shapes=[pltpu.VMEM((8, LANES), F32)],
        compiler_params=_cparams("arbitrary"),
        name="moe_router",
    )(h, w_r)


def _cast_on_expert_change(te_ref, pairs):
    i = pl.program_id(0)
    prev = te_ref[jnp.maximum(i - 1, 0)]

    @pl.when(jnp.logical_or(i == 0, te_ref[i] != prev))
    def _():
        for w_ref, wb_ref in pairs:
            wb_ref[...] = w_ref[0, 0].astype(wb_ref.dtype)


def _expert_up_kernel(te_ref, tv_ref, x_ref, wg_ref, wu_ref, o_ref, wgb_ref, wub_ref):
    i = pl.program_id(0)
    _cast_on_expert_change(te_ref, ((wg_ref, wgb_ref), (wu_ref, wub_ref)))

    @pl.when(tv_ref[i] > 0)
    def _():
        half = x_ref.shape[1]
        lo, hi = _unpack_pairs(x_ref[...])
        lo = lo.astype(BF16)
        hi = hi.astype(BF16)
        hg = jnp.dot(lo, wgb_ref[:half, :], preferred_element_type=F32)
        hg = hg + jnp.dot(hi, wgb_ref[half:, :], preferred_element_type=F32)
        hu = jnp.dot(lo, wub_ref[:half, :], preferred_element_type=F32)
        hu = hu + jnp.dot(hi, wub_ref[half:, :], preferred_element_type=F32)
        o_ref[...] = ((hg * _sigmoid(hg)) * hu).astype(o_ref.dtype)

    @pl.when(tv_ref[i] == 0)
    def _():
        o_ref[...] = jnp.zeros(o_ref.shape, o_ref.dtype)


def _expert_down_kernel(te_ref, tv_ref, a_ref, wd_ref, o_ref, wdb_ref):
    i = pl.program_id(0)
    _cast_on_expert_change(te_ref, ((wd_ref, wdb_ref),))

    @pl.when(tv_ref[i] > 0)
    def _():
        o_ref[...] = _pack_pairs(jnp.dot(a_ref[...], wdb_ref[...], preferred_element_type=F32))

    @pl.when(tv_ref[i] == 0)
    def _():
        o_ref[...] = jnp.zeros(o_ref.shape, o_ref.dtype)


def _expert_ffn(xs, tile_e, tile_valid, w_g, w_u, w_d, li, tm):
    p, half = xs.shape
    d = 2 * half
    f = w_g.shape[3]

    def wmap(i, te, tv):
        return (li, te[i], 0, 0)

    def rmap(i, te, tv):
        return (i, 0)

    act = pl.pallas_call(
        _expert_up_kernel,
        grid_spec=pltpu.PrefetchScalarGridSpec(
            num_scalar_prefetch=2,
            grid=(p // tm,),
            in_specs=[pl.BlockSpec((tm, half), rmap),
                      pl.BlockSpec((1, 1, d, f), wmap),
                      pl.BlockSpec((1, 1, d, f), wmap)],
            out_specs=pl.BlockSpec((tm, f), rmap),
            scratch_shapes=[pltpu.VMEM((d, f), BF16), pltpu.VMEM((d, f), BF16)]),
        out_shape=jax.ShapeDtypeStruct((p, f), BF16),
        compiler_params=_cparams("arbitrary"),
        name="moe_up",
    )(tile_e, tile_valid, xs, w_g, w_u)
    return pl.pallas_call(
        _expert_down_kernel,
        grid_spec=pltpu.PrefetchScalarGridSpec(
            num_scalar_prefetch=2,
            grid=(p // tm,),
            in_specs=[pl.BlockSpec((tm, f), rmap),
                      pl.BlockSpec((1, 1, f, d), wmap)],
            out_specs=pl.BlockSpec((tm, half), rmap),
            scratch_shapes=[pltpu.VMEM((f, d), BF16)]),
        out_shape=jax.ShapeDtypeStruct((p, half), jnp.uint32),
        compiler_params=_cparams("arbitrary"),
        name="moe_down",
    )(tile_e, tile_valid, act, w_d)


def _combine_kernel(n_lat, tm, tile0, z_ref, ya_ref, yb_ref, info_ref, gl_ref, gc_ref, o_ref):
    is_lat = (pl.program_id(0) + tile0) * tm < n_lat
    gate = jnp.where(is_lat, gl_ref[0], gc_ref[0])
    w1 = info_ref[:, R_WT:R_WT + 1]
    w2 = info_ref[:, R_WT + 1:R_WT + 2]
    half = ya_ref.shape[1]
    a_lo, a_hi = _unpack_pairs(ya_ref[...])
    b_lo, b_hi = _unpack_pairs(yb_ref[...])
    o_ref[:, :half] = z_ref[:, :half] + gate[:, :half] * (w1 * a_lo + w2 * b_lo)
    o_ref[:, half:] = z_ref[:, half:] + gate[:, half:] * (w1 * a_hi + w2 * b_hi)


def _combine(z, y2, info, mods, gate_idx, row0, part_rows, n_lat):
    d = z.shape[1]
    tm = MOE_TILE
    assert row0 % tm == 0 and part_rows % tm == 0 and n_lat % tm == 0
    nt = part_rows // tm
    t0 = row0 // tm
    return pl.pallas_call(
        functools.partial(_combine_kernel, n_lat, tm, t0),
        grid=(nt,),
        in_specs=[pl.BlockSpec((tm, d), lambda i: (i + t0, 0)),
                  pl.BlockSpec((tm, d // 2), lambda i: (i, 0)),
                  pl.BlockSpec((tm, d // 2), lambda i: (i + nt, 0)),
                  pl.BlockSpec((tm, LANES), lambda i: (i + t0, 0)),
                  pl.BlockSpec((1, 1, d), lambda i: (gate_idx, 0, 0)),
                  pl.BlockSpec((1, 1, d), lambda i: (N_MOD + gate_idx, 0, 0))],
        out_specs=pl.BlockSpec((tm, d), lambda i: (i + t0, 0)),
        out_shape=jax.ShapeDtypeStruct(z.shape, F32),
        input_output_aliases={0: 0},
        compiler_params=_cparams("parallel"),
        name="moe_combine",
    )(z, y2, y2, info, mods, mods)


def _rows_of(x, idx):
    return x.at[idx].get(mode="promise_in_bounds", unique_indices=False)


def _moe(h2, h2_pairs, z, mods, w_r, w_g, w_u, w_d, li, n_groups, per_group, rows, n_lat):
    n_exp = n_groups * per_group
    tm = MOE_TILE
    first = (rows // tm // MOE_PARTS) * tm
    part_starts = tuple(range(0, first * MOE_PARTS, first))
    info, cnt = _router(h2, w_r, n_groups, per_group, rows, part_starts)
    experts = jnp.arange(n_exp, dtype=jnp.int32)
    for part, row0 in enumerate(part_starts):
        part_rows = (part_starts[part + 1] if part + 1 < len(part_starts) else rows) - row0
        eid = info[row0:row0 + part_rows, R_EID:R_EID + EXPERT_TOPK].astype(jnp.int32)
        rank = info[row0:row0 + part_rows, R_RANK:R_RANK + EXPERT_TOPK].astype(jnp.int32)
        counts = cnt[8 * part, n_groups:n_groups + n_exp].astype(jnp.int32)
        padded = ((counts + tm - 1) // tm) * tm
        ends = jnp.sum(jnp.where(experts[None, :] <= experts[:, None], padded[None, :], 0), axis=1)
        starts = ends - padded
        pos = jnp.sum(jnp.where(eid[:, :, None] == experts[None, None, :], starts[None, None, :], 0),
                      axis=2) + rank
        pos = pos.T.reshape(-1)
        n_pairs = part_rows * EXPERT_TOPK
        n_tiles = (n_pairs + n_exp * (tm - 1)) // tm + 1
        p = n_tiles * tm
        pair_token = row0 + jnp.arange(n_pairs, dtype=jnp.int32) % part_rows
        row_token = jnp.zeros((p,), jnp.int32).at[pos].set(pair_token, mode="promise_in_bounds",
                                                          unique_indices=True)
        tile_start = jnp.arange(n_tiles, dtype=jnp.int32) * tm
        tile_e = jnp.minimum(jnp.sum((ends[None, :] <= tile_start[:, None]).astype(jnp.int32), axis=1),
                             n_exp - 1)
        tile_valid = (tile_start < ends[-1]).astype(jnp.int32)
        xs = _rows_of(h2_pairs, row_token)
        ys = _expert_ffn(xs, tile_e, tile_valid, w_g, w_u, w_d, li, tm)
        z = _combine(z, _rows_of(ys, pos), info, mods, 5, row0, part_rows, n_lat)
    return z


def _rope_tables(n_lat, n_ctx):
    t = jnp.arange(n_lat)
    row = (t // GRID_W).astype(F32)
    col = (t % GRID_W).astype(F32)
    inv = ROPE_THETA ** (-jnp.arange(ROPE_PAIRS, dtype=F32) / ROPE_PAIRS)
    ar = row[:, None] * inv
    ac = col[:, None] * inv
    cos = jnp.concatenate([jnp.cos(ar), jnp.cos(ar), jnp.cos(ac), jnp.cos(ac)], axis=1)
    sin = jnp.concatenate([-jnp.sin(ar), jnp.sin(ar), -jnp.sin(ac), jnp.sin(ac)], axis=1)
    cos = jnp.concatenate([cos, jnp.ones((n_ctx, HEAD_DIM), F32)], axis=0)
    sin = jnp.concatenate([sin, jnp.zeros((n_ctx, HEAD_DIM), F32)], axis=0)
    return cos, sin


def _with_ctx(o_lat, o_ctx, n_lat):
    return lax.dynamic_update_slice(o_lat, o_ctx, (n_lat, 0))


def kernel(x, c, ctx, c_ctx, ada_down, ada_up, ada_bias, norm1_g, norm2_g, w_in_ab, w_out_ab, qnorm_a,
           knorm_a, sink_b, w_in_cd, w_out_cd, rpb_c, dw_w, dw_b, ln_d_g, ln_d_b, router_group,
           router_expert, w_gate, w_up, w_down, final_g):
    assert x.shape[0] == 1 and ctx.shape[0] == 1
    n_lat, d = x.shape[1], x.shape[2]
    n_ctx = ctx.shape[1]
    n_all = n_lat + n_ctx
    depth = ada_down.shape[0]
    assert n_lat % n_ctx == 0 and n_ctx % 128 == 0
    n_groups, per_group = router_expert.shape[1], router_expert.shape[3]
    n_mix = d // HEAD_DIM
    a_w = (n_mix // 2) * HEAD_DIM
    akv_w = (n_mix // 8) * HEAD_DIM
    c_w = (n_mix // 2) * HEAD_DIM
    d_ch = d // 2
    qscale = HEAD_DIM ** -0.5 * LOG2E

    z = jnp.concatenate([x[0], ctx[0]], axis=0)
    cond = jnp.zeros((8, d), F32).at[0].set(c[0]).at[1].set(c_ctx)
    mods_all = _modulation_all(cond, ada_down, ada_up, ada_bias)
    tabs = _rope_tables(n_lat, n_ctx)

    for layer in range(depth):
        need_ctx = layer < depth - 1
        rows = n_all if need_ctx else n_lat
        mods = mods_all[layer, :2].reshape(2 * N_MOD, 1, d)
        h = _norm_mod(z, norm1_g[layer], mods, 0, 1, n_all, n_lat, BF16)
        i = layer // 2
        if layer % 2 == 0:
            w = w_in_ab
            sink = sink_b[i] * LOG2E
            col = 0
            qa = _proj_in(h, w, i, col, a_w, n_all, gain=qnorm_a[i], rope_tabs=tabs, scale=qscale)
            col += a_w
            ka = _proj_in(h, w, i, col, akv_w, n_all, gain=knorm_a[i], rope_tabs=tabs)
            col += akv_w
            va = _proj_in(h, w, i, col, akv_w, n_all, ones_ext=True)
            col += akv_w
            qb = _proj_in(h, w, i, col, a_w, n_all, rope_tabs=tabs, scale=qscale)
            col += a_w
            kb = _proj_in(h, w, i, col, akv_w, n_all, rope_tabs=tabs)
            col += akv_w
            vb = _proj_in(h, w, i, col, akv_w, n_all, ones_ext=True)
            o1 = _global_attn(qa, ka, va, n_lat, n_ctx, rows)
            o2 = _window_attn(qb, kb, vb, sink, n_lat, n_ctx, rows)
            if need_ctx:
                o1 = _with_ctx(o1, _ctx_attn(qa, ka, va, None, n_lat, n_ctx), n_lat)
                o2 = _with_ctx(o2, _ctx_attn(qb, kb, vb, sink, n_lat, n_ctx), n_lat)
            w_out = w_out_ab
        else:
            w = w_in_cd
            qc = _proj_in(h, w, i, 0, c_w, n_all, scale=qscale)
            kc = _proj_in(h, w, i, c_w, c_w, n_all)
            vc = _proj_in(h, w, i, 2 * c_w, c_w, n_all, ones_ext=True)
            u = _proj_glu(h, w, i, 3 * c_w, d_ch, n_all)
            bias = _na_bias_table(rpb_c[i], n_lat // GRID_W)
            o1 = _na_attn(qc, kc, vc, bias, n_lat, n_ctx, rows)
            if need_ctx:
                o1 = _with_ctx(o1, _ctx_attn(qc, kc, vc, None, n_lat, n_ctx), n_lat)
            o2 = _conformer_conv(u, dw_w[i], dw_b[i], ln_d_g[i], ln_d_b[i], rows, n_lat)
            w_out = w_out_cd
        z = _proj_out(o1, o2, w_out, i, z, mods, 2, rows, n_lat)
        h2, h2_pairs = _norm_mod(z, norm2_g[layer], mods, 3, 4, rows, n_lat, BF16, packed=True)
        w_r = jnp.concatenate(
            [router_group[layer], router_expert[layer].transpose(1, 0, 2).reshape(d, n_groups * per_group)],
            axis=1)
        w_r = jnp.pad(w_r, ((0, 0), (0, LANES - w_r.shape[1]))).astype(BF16)
        z = _moe(h2, h2_pairs, z, mods, w_r, w_gate, w_up, w_down, layer, n_groups, per_group, rows, n_lat)
    out = _norm_mod(z, final_g, None, 0, 0, n_lat, n_lat, F32)
    return out[None]
```

```python
import functools
import math

import numpy as np
import jax
import jax.numpy as jnp
from jax import lax
from jax.experimental import pallas as pl
from jax.experimental.pallas import tpu as pltpu

HEAD_DIM = 128
GRID_W = 64
ROPE_PAIRS = HEAD_DIM // 4
ROPE_THETA = 10000.0
WINDOW = 128
NA_KH = 8
NA_KW = 16
CONV_W = 31
N_MOD = 6
EXPERT_TOPK = 2
EPS = 1e-6
NEG = -1e30
LOG2E = math.log2(math.e)
LANES = 128
VMEM_LIMIT = 56 * 1024 * 1024
NA_ROWS = 8
CONV_HALO = 16
EXT = 2 * HEAD_DIM
MOE_TILE = 256
MOE_PARTS = 1

BF16 = jnp.bfloat16
F32 = jnp.float32


def _cparams(*sem):
    return pltpu.CompilerParams(dimension_semantics=sem, vmem_limit_bytes=VMEM_LIMIT)


def _row_tile(m, cap):
    best = None
    for t in range(LANES, min(m, cap) + 1, LANES):
        if m % t == 0:
            best = t
    assert best is not None, (m, cap)
    return best


def _sigmoid(x):
    return 1.0 / (1.0 + jnp.exp(-x))


def _mod_down_kernel(cond_ref, down_ref, o_ref):
    cnd = cond_ref[...]
    a = cnd * _sigmoid(cnd)
    o_ref[0] = jnp.dot(a, down_ref[0], preferred_element_type=F32, precision=lax.Precision.HIGHEST)


def _mod_up_kernel(t_ref, up_ref, b_ref, o_ref):
    o_ref[0] = jnp.dot(t_ref[0], up_ref[0], preferred_element_type=F32,
                       precision=lax.Precision.HIGHEST) + b_ref[0]


def _modulation_all(cond, ada_down, ada_up, ada_bias):
    depth, d, r = ada_down.shape
    n = ada_up.shape[2]
    t = pl.pallas_call(
        _mod_down_kernel,
        grid=(depth,),
        in_specs=[pl.BlockSpec((8, d), lambda l: (0, 0)),
                  pl.BlockSpec((1, d, r), lambda l: (l, 0, 0))],
        out_specs=pl.BlockSpec((1, 8, r), lambda l: (l, 0, 0)),
        out_shape=jax.ShapeDtypeStruct((depth, 8, r), F32),
        compiler_params=_cparams("arbitrary"),
        name="mod_down",
    )(cond, ada_down)
    tn = _row_tile(n, 2048)
    return pl.pallas_call(
        _mod_up_kernel,
        grid=(depth, n // tn),
        in_specs=[pl.BlockSpec((1, 8, r), lambda l, j: (l, 0, 0)),
                  pl.BlockSpec((1, r, tn), lambda l, j: (l, 0, j)),
                  pl.BlockSpec((1, 1, tn), lambda l, j: (l, 0, j))],
        out_specs=pl.BlockSpec((1, 8, tn), lambda l, j: (l, 0, j)),
        out_shape=jax.ShapeDtypeStruct((depth, 8, n), F32),
        compiler_params=_cparams("arbitrary", "arbitrary"),
        name="mod_up",
    )(t, ada_up, ada_bias.reshape(depth, 1, n))


def _pack_pairs(x):
    half = x.shape[1] // 2
    lo = lax.bitcast_convert_type(x[:, :half].astype(BF16).astype(F32), jnp.uint32)
    hi = lax.bitcast_convert_type(x[:, half:].astype(BF16).astype(F32), jnp.uint32)
    return lax.shift_right_logical(lo, jnp.uint32(16)) | (hi & jnp.uint32(0xFFFF0000))


def _unpack_pairs(w):
    lo = lax.bitcast_convert_type(lax.shift_left(w, jnp.uint32(16)), F32)
    hi = lax.bitcast_convert_type(w & jnp.uint32(0xFFFF0000), F32)
    return lo, hi


def _norm_mod_kernel(n_lat, tm, modulated, packed, z_ref, g_ref, *rest):
    rest = list(rest)
    p_ref = rest.pop() if packed else None
    if modulated:
        sh_l, sc_l, sh_c, sc_c, o_ref = rest
    else:
        (o_ref,) = rest
    x = z_ref[...]
    r = lax.rsqrt(jnp.mean(x * x, axis=-1, keepdims=True) + EPS)
    if modulated:
        is_lat = pl.program_id(0) * tm < n_lat
        shift = jnp.where(is_lat, sh_l[0], sh_c[0])
        scale = jnp.where(is_lat, sc_l[0], sc_c[0])
        h = (x * r) * (g_ref[...] * (1.0 + scale)) + shift
    else:
        h = x * r * g_ref[...]
    o_ref[...] = h.astype(o_ref.dtype)
    if packed:
        p_ref[...] = _pack_pairs(h)


def _norm_mod(z, g, mods, a, b, rows, n_lat, out_dtype, packed=False):
    d = z.shape[1]
    tm = _row_tile(math.gcd(rows, n_lat), 256)
    modulated = mods is not None
    in_specs = [pl.BlockSpec((tm, d), lambda i: (i, 0)),
                pl.BlockSpec((1, d), lambda i: (0, 0))]
    args = [z, g.reshape(1, d)]
    if modulated:
        for idx in (a, b, N_MOD + a, N_MOD + b):
            in_specs.append(pl.BlockSpec((1, 1, d), lambda i, idx=idx: (idx, 0, 0)))
            args.append(mods)
    out_specs = pl.BlockSpec((tm, d), lambda i: (i, 0))
    out_shape = jax.ShapeDtypeStruct((rows, d), out_dtype)
    if packed:
        out_specs = [out_specs, pl.BlockSpec((tm, d // 2), lambda i: (i, 0))]
        out_shape = [out_shape, jax.ShapeDtypeStruct((rows, d // 2), jnp.uint32)]
    return pl.pallas_call(
        functools.partial(_norm_mod_kernel, n_lat, tm, modulated, packed),
        grid=(rows // tm,),
        in_specs=in_specs,
        out_specs=out_specs,
        out_shape=out_shape,
        compiler_params=_cparams("parallel"),
        name="norm_mod",
    )(*args)


def _rope_swap(x):
    lane = lax.broadcasted_iota(jnp.int32, x.shape, 1)
    first_half = (lane % (2 * ROPE_PAIRS)) < ROPE_PAIRS
    return jnp.where(first_half, pltpu.roll(x, HEAD_DIM - ROPE_PAIRS, 1), pltpu.roll(x, ROPE_PAIRS, 1))


def _cast_weight_tile(w_ref, wb_ref):
    @pl.when(pl.program_id(1) == 0)
    def _():
        wb_ref[...] = w_ref[0].astype(wb_ref.dtype)


def _proj_in_kernel(norm, rope, scale, ones_ext, heads, a_ref, w_ref, *rest):
    rest = list(rest)
    g_ref = rest.pop(0) if norm else None
    if rope:
        c_ref = rest.pop(0)
        s_ref = rest.pop(0)
    o_ref, wb_ref = rest
    _cast_weight_tile(w_ref, wb_ref)
    acc = jnp.dot(a_ref[...], wb_ref[...], preferred_element_type=F32)
    if not (norm or rope or ones_ext):
        if scale != 1.0:
            acc = acc * scale
        o_ref[...] = acc.astype(o_ref.dtype)
        return
    ow = EXT if ones_ext else HEAD_DIM
    for hh in range(heads):
        x = acc[:, hh * HEAD_DIM:(hh + 1) * HEAD_DIM]
        if norm:
            r = lax.rsqrt(jnp.mean(x * x, axis=-1, keepdims=True) + EPS)
            x = x * r * g_ref[...]
        if rope:
            x = x * c_ref[...] + _rope_swap(x) * s_ref[...]
        if scale != 1.0:
            x = x * scale
        o_ref[:, hh * ow:hh * ow + HEAD_DIM] = x.astype(o_ref.dtype)
        if ones_ext:
            o_ref[:, hh * ow + HEAD_DIM:(hh + 1) * ow] = jnp.ones(x.shape, o_ref.dtype)


def _proj_in(h, w, li, col0, width, rows, *, gain=None, rope_tabs=None, scale=1.0, ones_ext=False):
    k = h.shape[1]
    tm = _row_tile(rows, 1280)
    tn = math.gcd(math.gcd(width, 512), col0) if col0 else math.gcd(width, 512)
    norm = gain is not None
    rope = rope_tabs is not None
    in_specs = [pl.BlockSpec((tm, k), lambda j, i: (i, 0)),
                pl.BlockSpec((1, k, tn), lambda j, i: (li, 0, col0 // tn + j))]
    args = [h, w]
    if norm:
        in_specs.append(pl.BlockSpec((1, HEAD_DIM), lambda j, i: (0, 0)))
        args.append(gain.reshape(1, HEAD_DIM))
    if rope:
        for tab in rope_tabs:
            in_specs.append(pl.BlockSpec((tm, HEAD_DIM), lambda j, i: (i, 0)))
            args.append(tab)
    mult = 2 if ones_ext else 1
    return pl.pallas_call(
        functools.partial(_proj_in_kernel, norm, rope, scale, ones_ext, tn // HEAD_DIM),
        grid=(width // tn, rows // tm),
        in_specs=in_specs,
        out_specs=pl.BlockSpec((tm, mult * tn), lambda j, i: (i, j)),
        out_shape=jax.ShapeDtypeStruct((rows, mult * width), BF16),
        scratch_shapes=[pltpu.VMEM((k, tn), BF16)],
        compiler_params=_cparams("parallel", "arbitrary"),
        name="proj_in",
    )(*args)


def _proj_glu_kernel(a_ref, wa_ref, wg_ref, o_ref, wab_ref, wgb_ref):
    _cast_weight_tile(wa_ref, wab_ref)
    _cast_weight_tile(wg_ref, wgb_ref)
    a = a_ref[...]
    va = jnp.dot(a, wab_ref[...], preferred_element_type=F32)
    vg = jnp.dot(a, wgb_ref[...], preferred_element_type=F32)
    o_ref[...] = (va * _sigmoid(vg)).astype(o_ref.dtype)


def _proj_glu(h, w, li, col0, width, rows):
    k = h.shape[1]
    tm = _row_tile(rows, 1280)
    tn = math.gcd(math.gcd(width, 256), col0)
    return pl.pallas_call(
        _proj_glu_kernel,
        grid=(width // tn, rows // tm),
        in_specs=[pl.BlockSpec((tm, k), lambda j, i: (i, 0)),
                  pl.BlockSpec((1, k, tn), lambda j, i: (li, 0, col0 // tn + j)),
                  pl.BlockSpec((1, k, tn), lambda j, i: (li, 0, (col0 + width) // tn + j))],
        out_specs=pl.BlockSpec((tm, tn), lambda j, i: (i, j)),
        out_shape=jax.ShapeDtypeStruct((rows, width), F32),
        scratch_shapes=[pltpu.VMEM((k, tn), BF16), pltpu.VMEM((k, tn), BF16)],
        compiler_params=_cparams("parallel", "arbitrary"),
        name="proj_glu",
    )(h, w, w)


def _proj_out_kernel(n_lat, tm, a1_ref, a2_ref, w1_ref, w2_ref, z_ref, gl_ref, gc_ref, o_ref,
                     w1b_ref, w2b_ref):
    _cast_weight_tile(w1_ref, w1b_ref)
    _cast_weight_tile(w2_ref, w2b_ref)
    acc = jnp.dot(a1_ref[...], w1b_ref[...], preferred_element_type=F32)
    acc = acc + jnp.dot(a2_ref[...], w2b_ref[...], preferred_element_type=F32)
    rows = pl.program_id(1) * tm + lax.broadcasted_iota(jnp.int32, (tm, 1), 0)
    gate = jnp.where(rows < n_lat, gl_ref[0], gc_ref[0])
    o_ref[...] = z_ref[...] + gate * acc


def _proj_out(a1, a2, w, li, z, mods, gate_idx, rows, n_lat):
    k1, k2 = a1.shape[1], a2.shape[1]
    assert k1 == k2
    n = w.shape[2]
    tm = _row_tile(rows, 1280)
    tn = math.gcd(n, 512)
    return pl.pallas_call(
        functools.partial(_proj_out_kernel, n_lat, tm),
        grid=(n // tn, rows // tm),
        in_specs=[pl.BlockSpec((tm, k1), lambda j, i: (i, 0)),
                  pl.BlockSpec((tm, k2), lambda j, i: (i, 0)),
                  pl.BlockSpec((1, k1, tn), lambda j, i: (li, 0, j)),
                  pl.BlockSpec((1, k2, tn), lambda j, i: (li, 1, j)),
                  pl.BlockSpec((tm, tn), lambda j, i: (i, j)),
                  pl.BlockSpec((1, 1, tn), lambda j, i: (gate_idx, 0, j)),
                  pl.BlockSpec((1, 1, tn), lambda j, i: (N_MOD + gate_idx, 0, j))],
        out_specs=pl.BlockSpec((tm, tn), lambda j, i: (i, j)),
        out_shape=jax.ShapeDtypeStruct((rows, n), F32),
        scratch_shapes=[pltpu.VMEM((k1, tn), BF16), pltpu.VMEM((k2, tn), BF16)],
        compiler_params=_cparams("parallel", "arbitrary"),
        name="proj_out",
    )(a1, a2, w, w, z, mods, mods)


def _nt_dot(a, b):
    return lax.dot_general(a, b, (((1,), (1,)), ((), ())), preferred_element_type=F32)


def _softmax_pv(s, v_ext, m_ref, acc_ref):
    m_prev = m_ref[...]
    m_new = jnp.maximum(m_prev, jnp.max(s, axis=1, keepdims=True))
    alpha = jnp.exp2(m_prev - m_new)
    p = jnp.exp2(s - jnp.tile(m_new, (1, s.shape[1] // LANES)))
    pv = jnp.dot(p.astype(v_ext.dtype), v_ext, preferred_element_type=F32)
    acc_ref[...] = jnp.tile(alpha, (1, EXT // LANES)) * acc_ref[...] + pv
    m_ref[...] = m_new


def _softmax_chunk(q, k, v_ext, m_ref, acc_ref, mask=None):
    s = _nt_dot(q, k)
    if mask is not None:
        s = jnp.where(mask, s, NEG)
    _softmax_pv(s, v_ext, m_ref, acc_ref)


def _stack_heads(groups, tq, q_ref, qs_ref):
    for g in range(groups):
        qs_ref[g * tq:(g + 1) * tq, :] = q_ref[:, g * HEAD_DIM:(g + 1) * HEAD_DIM]


def _write_heads(groups, tq, acc_ref, o_ref):
    for g in range(groups):
        a = acc_ref[g * tq:(g + 1) * tq, :]
        o_ref[:, g * HEAD_DIM:(g + 1) * HEAD_DIM] = (a[:, :HEAD_DIM] / a[:, HEAD_DIM:]).astype(o_ref.dtype)


def _global_attn_kernel(groups, tq, kc, n_chunks, q_ref, k_ref, v_ref, kc_ref, vc_ref, o_ref,
                        qs_ref, m_ref, acc_ref, s0_ref, s1_ref):
    _stack_heads(groups, tq, q_ref, qs_ref)
    m_ref[...] = jnp.full(m_ref.shape, NEG, F32)
    acc_ref[...] = jnp.zeros(acc_ref.shape, F32)
    _softmax_chunk(qs_ref[...], kc_ref[...], vc_ref[...], m_ref, acc_ref)

    def logits(c):
        return _nt_dot(qs_ref[...], k_ref[pl.ds(pl.multiple_of(c * kc, kc), kc), :])

    def update(s_ref, c):
        _softmax_pv(s_ref[...], v_ref[pl.ds(pl.multiple_of(c * kc, kc), kc), :], m_ref, acc_ref)

    s0_ref[...] = logits(0)

    def body(t, carry):
        c = 2 * t
        s1_ref[...] = logits(c + 1)
        update(s0_ref, c)
        s0_ref[...] = logits(c + 2)
        update(s1_ref, c + 1)
        return carry

    lax.fori_loop(0, n_chunks // 2 - 1, body, 0)
    s1_ref[...] = logits(n_chunks - 1)
    update(s0_ref, n_chunks - 2)
    update(s1_ref, n_chunks - 1)
    _write_heads(groups, tq, acc_ref, o_ref)


def _global_attn(q, k, v_ext, n_lat, n_ctx, out_rows):
    hq = q.shape[1] // HEAD_DIM
    hkv = k.shape[1] // HEAD_DIM
    groups = hq // hkv
    tq = _row_tile(n_lat, 256)
    kc = _row_tile(n_lat, 512)
    n_chunks = n_lat // kc
    assert n_chunks >= 2 and n_chunks % 2 == 0
    gw = groups * HEAD_DIM
    ctx_blk = n_lat // n_ctx
    rows = groups * tq
    return pl.pallas_call(
        functools.partial(_global_attn_kernel, groups, tq, kc, n_chunks),
        grid=(hkv, n_lat // tq),
        in_specs=[pl.BlockSpec((tq, gw), lambda h, i: (i, h)),
                  pl.BlockSpec((n_lat, HEAD_DIM), lambda h, i: (0, h)),
                  pl.BlockSpec((n_lat, EXT), lambda h, i: (0, h)),
                  pl.BlockSpec((n_ctx, HEAD_DIM), lambda h, i: (ctx_blk, h)),
                  pl.BlockSpec((n_ctx, EXT), lambda h, i: (ctx_blk, h))],
        out_specs=pl.BlockSpec((tq, gw), lambda h, i: (i, h)),
        out_shape=jax.ShapeDtypeStruct((out_rows, q.shape[1]), BF16),
        scratch_shapes=[pltpu.VMEM((rows, HEAD_DIM), BF16),
                        pltpu.VMEM((rows, LANES), F32),
                        pltpu.VMEM((rows, EXT), F32),
                        pltpu.VMEM((rows, kc), F32),
                        pltpu.VMEM((rows, kc), F32)],
        compiler_params=_cparams("parallel", "arbitrary"),
        name="global_attn",
    )(q, k, v_ext, k, v_ext)


def _window_attn_kernel(groups, tq, span, n_lat, sink_ref, q_ref, k_ref, v_ref, kc_ref, vc_ref, o_ref,
                        qs_ref, m_ref, acc_ref):
    h = pl.program_id(0)
    i = pl.program_id(1)
    rows = groups * tq
    _stack_heads(groups, tq, q_ref, qs_ref)
    for g in range(groups):
        m_ref[g * tq:(g + 1) * tq, :] = jnp.full((tq, LANES), sink_ref[h * groups + g], F32)
    lane = lax.broadcasted_iota(jnp.int32, (rows, EXT), 1)
    acc_ref[...] = jnp.where(lane >= HEAD_DIM, 1.0, 0.0).astype(F32)
    _softmax_chunk(qs_ref[...], kc_ref[...], vc_ref[...], m_ref, acc_ref)
    start = pl.multiple_of(jnp.clip(i * tq - WINDOW, 0, n_lat - span), WINDOW)
    qpos = i * tq + lax.broadcasted_iota(jnp.int32, (tq, span), 0)
    kpos = start + lax.broadcasted_iota(jnp.int32, (tq, span), 1)
    band = jnp.where(jnp.abs(qpos - kpos) <= WINDOW, 0.0, NEG)
    s = _nt_dot(qs_ref[...], k_ref[pl.ds(start, span), :])
    s = (s.reshape(groups, tq, span) + band[None]).reshape(rows, span)
    _softmax_pv(s, v_ref[pl.ds(start, span), :], m_ref, acc_ref)
    _write_heads(groups, tq, acc_ref, o_ref)


def _window_attn(q, k, v_ext, sink, n_lat, n_ctx, out_rows):
    hq = q.shape[1] // HEAD_DIM
    hkv = k.shape[1] // HEAD_DIM
    groups = hq // hkv
    tq = _row_tile(n_lat, 256)
    span = tq + 2 * WINDOW
    assert tq % WINDOW == 0 and n_lat >= span
    gw = groups * HEAD_DIM
    ctx_blk = n_lat // n_ctx
    rows = groups * tq
    return pl.pallas_call(
        functools.partial(_window_attn_kernel, groups, tq, span, n_lat),
        grid=(hkv, n_lat // tq),
        in_specs=[pl.BlockSpec(memory_space=pltpu.SMEM),
                  pl.BlockSpec((tq, gw), lambda h, i: (i, h)),
                  pl.BlockSpec((n_lat, HEAD_DIM), lambda h, i: (0, h)),
                  pl.BlockSpec((n_lat, EXT), lambda h, i: (0, h)),
                  pl.BlockSpec((n_ctx, HEAD_DIM), lambda h, i: (ctx_blk, h)),
                  pl.BlockSpec((n_ctx, EXT), lambda h, i: (ctx_blk, h))],
        out_specs=pl.BlockSpec((tq, gw), lambda h, i: (i, h)),
        out_shape=jax.ShapeDtypeStruct((out_rows, q.shape[1]), BF16),
        scratch_shapes=[pltpu.VMEM((rows, HEAD_DIM), BF16),
                        pltpu.VMEM((rows, LANES), F32),
                        pltpu.VMEM((rows, EXT), F32)],
        compiler_params=_cparams("parallel", "arbitrary"),
        name="window_attn",
    )(sink, q, k, v_ext, k, v_ext)


def _ctx_attn_kernel(groups, has_sink, *refs):
    if has_sink:
        sink_ref, q_ref, k_ref, v_ref, o_ref = refs
    else:
        q_ref, k_ref, v_ref, o_ref = refs
    h = pl.program_id(0)
    for g in range(groups):
        q = q_ref[:, g * HEAD_DIM:(g + 1) * HEAD_DIM]
        s = _nt_dot(q, k_ref[...])
        m = jnp.max(s, axis=-1, keepdims=True)
        if has_sink:
            snk = sink_ref[h * groups + g]
            m = jnp.maximum(m, snk)
        p = jnp.exp2(s - m)
        o = jnp.dot(p.astype(v_ref.dtype), v_ref[...], preferred_element_type=F32)
        l = o[:, HEAD_DIM:]
        if has_sink:
            l = l + jnp.exp2(snk - m)
        o_ref[:, g * HEAD_DIM:(g + 1) * HEAD_DIM] = (o[:, :HEAD_DIM] / l).astype(o_ref.dtype)


def _ctx_attn(q, k, v_ext, sink, n_lat, n_ctx):
    hq = q.shape[1] // HEAD_DIM
    hkv = k.shape[1] // HEAD_DIM
    groups = hq // hkv
    gw = groups * HEAD_DIM
    ctx_blk = n_lat // n_ctx
    has_sink = sink is not None
    in_specs = [pl.BlockSpec((n_ctx, gw), lambda h: (ctx_blk, h)),
                pl.BlockSpec((n_ctx, HEAD_DIM), lambda h: (ctx_blk, h)),
                pl.BlockSpec((n_ctx, EXT), lambda h: (ctx_blk, h))]
    args = [q, k, v_ext]
    if has_sink:
        in_specs.insert(0, pl.BlockSpec(memory_space=pltpu.SMEM))
        args.insert(0, sink)
    return pl.pallas_call(
        functools.partial(_ctx_attn_kernel, groups, has_sink),
        grid=(hkv,),
        in_specs=in_specs,
        out_specs=pl.BlockSpec((n_ctx, gw), lambda h: (0, h)),
        out_shape=jax.ShapeDtypeStruct((n_ctx, q.shape[1]), BF16),
        compiler_params=_cparams("parallel"),
        name="ctx_attn",
    )(*args)


def _na_bias_table(rpb, n_rows):
    kh = min(NA_KH, n_rows)
    assert kh == NA_KH and NA_ROWS == NA_KH and n_rows % NA_ROWS == 0 and n_rows // NA_ROWS >= 2
    half = NA_ROWS // 2
    qr = np.arange(NA_ROWS)
    krel = np.where(qr[:, None] < half, np.arange(-half, 0)[None, :], NA_ROWS + np.arange(half)[None, :])
    krel = np.concatenate([krel, np.broadcast_to(np.arange(NA_ROWS), (NA_ROWS, NA_ROWS))], axis=1)
    nk = krel.shape[1]
    ri = np.zeros((3, NA_ROWS, nk), np.int32)
    rvalid = np.zeros((3, NA_ROWS, nk), bool)
    big = 4 * NA_ROWS
    for var, (r0, rows) in enumerate(((0, big), (NA_ROWS, big), (n_rows - NA_ROWS, n_rows))):
        r = r0 + qr[:, None]
        key = r0 + krel
        rs = np.clip(r - kh // 2, 0, rows - kh)
        rvalid[var] = (key >= rs) & (key < rs + kh)
        ri[var] = np.clip(key - r + NA_KH - 1, 0, 2 * NA_KH - 2)
    qc = np.arange(GRID_W)
    cs = np.clip(qc - NA_KW // 2, 0, GRID_W - NA_KW)
    kc = np.arange(GRID_W)
    cvalid = (kc[None, :] >= cs[:, None]) & (kc[None, :] < cs[:, None] + NA_KW)
    ci = np.clip(kc[None, :] - qc[:, None] + NA_KW - 1, 0, 2 * NA_KW - 2)
    t = jnp.take(rpb * LOG2E, jnp.asarray(ci), axis=2)
    t = t[:, jnp.asarray(ri)]
    valid = rvalid[:, :, :, None, None] & cvalid[None, None, None, :, :]
    t = jnp.where(jnp.asarray(valid)[None], t, NEG)
    t = t.transpose(0, 1, 2, 4, 3, 5)
    return t.reshape(rpb.shape[0], 3, NA_ROWS * GRID_W, nk * GRID_W)


def _na_kernel(nq, q_ref, kp_ref, k0_ref, kn_ref, vp_ref, v0_ref, vn_ref, kc_ref, vc_ref, b_ref, o_ref):
    half = nq // 2
    q = q_ref[...]
    s_nb = jnp.concatenate([_nt_dot(q[:half], kp_ref[...]), _nt_dot(q[half:], kn_ref[...])], axis=0)
    s_nb = s_nb + b_ref[0, 0, :, :half]
    s_own = _nt_dot(q, k0_ref[...]) + b_ref[0, 0, :, half:]
    s_ctx = _nt_dot(q, kc_ref[...])
    m = jnp.maximum(jnp.maximum(s_nb.max(axis=-1, keepdims=True), s_own.max(axis=-1, keepdims=True)),
                    s_ctx.max(axis=-1, keepdims=True))
    p_nb = jnp.exp2(s_nb - m).astype(vp_ref.dtype)
    p_own = jnp.exp2(s_own - m).astype(v0_ref.dtype)
    p_ctx = jnp.exp2(s_ctx - m).astype(vc_ref.dtype)
    o = jnp.concatenate([jnp.dot(p_nb[:half], vp_ref[...], preferred_element_type=F32),
                         jnp.dot(p_nb[half:], vn_ref[...], preferred_element_type=F32)], axis=0)
    o = o + jnp.dot(p_own, v0_ref[...], preferred_element_type=F32)
    o = o + jnp.dot(p_ctx, vc_ref[...], preferred_element_type=F32)
    o_ref[...] = (o[:, :HEAD_DIM] / o[:, HEAD_DIM:]).astype(o_ref.dtype)


def _na_attn(q, k, v_ext, bias, n_lat, n_ctx, out_rows):
    heads = q.shape[1] // HEAD_DIM
    nq = NA_ROWS * GRID_W
    half = nq // 2
    nb = n_lat // nq
    ctx_blk = n_lat // n_ctx

    def prev_map(h, b):
        return (jnp.maximum(2 * b - 1, 0), h)

    def next_map(h, b):
        return (jnp.minimum(2 * b + 2, 2 * nb - 1), h)

    def own_map(h, b):
        return (b, h)

    def ctx_map(h, b):
        return (ctx_blk, h)

    def bias_map(h, b):
        return (h, jnp.where(b == 0, 0, jnp.where(b == nb - 1, 2, 1)), 0, 0)

    return pl.pallas_call(
        functools.partial(_na_kernel, nq),
        grid=(heads, nb),
        in_specs=[pl.BlockSpec((nq, HEAD_DIM), own_map),
                  pl.BlockSpec((half, HEAD_DIM), prev_map), pl.BlockSpec((nq, HEAD_DIM), own_map),
                  pl.BlockSpec((half, HEAD_DIM), next_map),
                  pl.BlockSpec((half, EXT), prev_map), pl.BlockSpec((nq, EXT), own_map),
                  pl.BlockSpec((half, EXT), next_map),
                  pl.BlockSpec((n_ctx, HEAD_DIM), ctx_map), pl.BlockSpec((n_ctx, EXT), ctx_map),
                  pl.BlockSpec((1, 1, nq, half + nq), bias_map)],
        out_specs=pl.BlockSpec((nq, HEAD_DIM), own_map),
        out_shape=jax.ShapeDtypeStruct((out_rows, q.shape[1]), BF16),
        compiler_params=_cparams("parallel", "arbitrary"),
        name="na_attn",
    )(q, k, k, k, v_ext, v_ext, v_ext, k, v_ext, bias)


def _conv_kernel(tr, n_lat_tiles, n_tiles, ch_chunk, prev_ref, cur_ref, next_ref, w_ref, b_ref,
                 g_ref, beta_ref, o_ref, win_ref, y_ref):
    i = pl.program_id(0)
    c = cur_ref.shape[1]
    prev_ok = jnp.logical_and(i != 0, i != n_lat_tiles)
    next_ok = jnp.logical_and(i != n_lat_tiles - 1, i != n_tiles - 1)
    win_ref[0:CONV_HALO, :] = jnp.where(prev_ok, prev_ref[...], 0.0)
    win_ref[CONV_HALO:CONV_HALO + tr, :] = cur_ref[...]
    win_ref[CONV_HALO + tr:, :] = jnp.where(next_ok, next_ref[...], 0.0)
    off = CONV_HALO - CONV_W // 2
    rc = 64

    def col_body(cc, carry):
        c0 = pl.multiple_of(cc * ch_chunk, ch_chunk)
        for r0 in range(0, tr, rc):
            acc = jnp.zeros((rc, ch_chunk), F32)
            for t in range(CONV_W):
                acc = acc + win_ref[r0 + off + t:r0 + off + t + rc, pl.ds(c0, ch_chunk)] * \
                    w_ref[t:t + 1, pl.ds(c0, ch_chunk)]
            y_ref[r0:r0 + rc, pl.ds(c0, ch_chunk)] = acc + b_ref[:, pl.ds(c0, ch_chunk)]
        return carry

    lax.fori_loop(0, c // ch_chunk, col_body, 0)
    y = y_ref[...]
    mu = jnp.mean(y, axis=-1, keepdims=True)
    var = jnp.mean(jnp.square(y - mu), axis=-1, keepdims=True)
    yn = (y - mu) * lax.rsqrt(var + EPS) * g_ref[...] + beta_ref[...]
    o_ref[...] = (yn * _sigmoid(yn)).astype(o_ref.dtype)


def _conformer_conv(u, dw_w, dw_b, ln_g, ln_b, rows, n_lat):
    c = u.shape[1]
    tr = 256
    assert n_lat % tr == 0 and rows % tr == 0
    hb = tr // CONV_HALO
    n_tiles = rows // tr
    last_halo = rows // CONV_HALO - 1
    ch_chunk = math.gcd(c, 256)
    return pl.pallas_call(
        functools.partial(_conv_kernel, tr, n_lat // tr, n_tiles, ch_chunk),
        grid=(n_tiles,),
        in_specs=[pl.BlockSpec((CONV_HALO, c), lambda i: (jnp.maximum(i * hb - 1, 0), 0)),
                  pl.BlockSpec((tr, c), lambda i: (i, 0)),
                  pl.BlockSpec((CONV_HALO, c), lambda i: (jnp.minimum((i + 1) * hb, last_halo), 0)),
                  pl.BlockSpec((CONV_W, c), lambda i: (0, 0)),
                  pl.BlockSpec((1, c), lambda i: (0, 0)),
                  pl.BlockSpec((1, c), lambda i: (0, 0)),
                  pl.BlockSpec((1, c), lambda i: (0, 0))],
        out_specs=pl.BlockSpec((tr, c), lambda i: (i, 0)),
        out_shape=jax.ShapeDtypeStruct((rows, c), BF16),
        scratch_shapes=[pltpu.VMEM((tr + 2 * CONV_HALO, c), F32),
                        pltpu.VMEM((tr, c), F32)],
        compiler_params=_cparams("parallel"),
        name="conformer_conv",
    )(u, u, u, dw_w, dw_b.reshape(1, c), ln_g.reshape(1, c), ln_b.reshape(1, c))


R_EID, R_WT, R_RANK = 0, 2, 4


def _part_of_tile(i, bounds):
    return sum((i >= b).astype(jnp.int32) for b in bounds[1:]) if len(bounds) > 1 else 0 * i


def _router_kernel(n_groups, per_group, tm, bounds, h_ref, w_ref, info_ref, cnt_ref, carry_ref):
    i = pl.program_id(0)
    starts_part = functools.reduce(jnp.logical_or, [i == b for b in bounds])

    @pl.when(starts_part)
    def _():
        carry_ref[...] = jnp.zeros(carry_ref.shape, F32)

    lg = jnp.dot(h_ref[...], w_ref[...], preferred_element_type=F32)
    lane = lax.broadcasted_iota(jnp.int32, lg.shape, 1).astype(F32)
    is_g = lane < n_groups
    gl = jnp.where(is_g, lg, NEG)
    mg = jnp.max(gl, axis=-1, keepdims=True)
    gsel = jnp.min(jnp.where(gl == mg, lane, float(LANES)), axis=-1, keepdims=True)
    gsum = jnp.sum(jnp.where(is_g, jnp.exp(gl - mg), 0.0), axis=-1, keepdims=True)
    lo = n_groups + gsel * per_group
    el = jnp.where((lane >= lo) & (lane < lo + per_group), lg, NEG)
    v1 = jnp.max(el, axis=-1, keepdims=True)
    i1 = jnp.min(jnp.where(el == v1, lane, float(LANES)), axis=-1, keepdims=True)
    el2 = jnp.where(lane == i1, NEG, el)
    v2 = jnp.max(el2, axis=-1, keepdims=True)
    i2 = jnp.min(jnp.where(el2 == v2, lane, float(LANES)), axis=-1, keepdims=True)
    e2 = jnp.exp(v2 - v1)
    w1 = 1.0 / ((1.0 + e2) * gsum)
    w2 = e2 * w1
    hit1 = lane == i1
    hit2 = lane == i2
    cnt = jnp.where(hit1 | hit2, 1.0, 0.0)
    tri = (lax.broadcasted_iota(jnp.int32, (tm, tm), 1) < lax.broadcasted_iota(jnp.int32, (tm, tm), 0))
    before = jnp.dot(jnp.where(tri, 1.0, 0.0).astype(BF16), cnt.astype(BF16), preferred_element_type=F32)
    tot = carry_ref[0:1, :] + before
    r1 = jnp.sum(jnp.where(hit1, tot, 0.0), axis=-1, keepdims=True)
    r2 = jnp.sum(jnp.where(hit2, tot, 0.0), axis=-1, keepdims=True)
    carry_ref[...] = carry_ref[...] + jnp.sum(cnt, axis=0, keepdims=True)
    info = jnp.zeros(lg.shape, F32)
    for col, val in ((R_EID, i1 - n_groups), (R_EID + 1, i2 - n_groups), (R_WT, w1), (R_WT + 1, w2),
                     (R_RANK, r1), (R_RANK + 1, r2)):
        info = jnp.where(lane == col, val, info)
    info_ref[...] = info
    cnt_ref[...] = carry_ref[...]


def _router(h, w_r, n_groups, per_group, rows, part_starts):
    d = h.shape[1]
    tm = MOE_TILE
    assert all(s % tm == 0 for s in part_starts) and rows % tm == 0
    bounds = tuple(s // tm for s in part_starts)
    return pl.pallas_call(
        functools.partial(_router_kernel, n_groups, per_group, tm, bounds),
        grid=(rows // tm,),
        in_specs=[pl.BlockSpec((tm, d), lambda i: (i, 0)),
                  pl.BlockSpec((d, LANES), lambda i: (0, 0))],
        out_specs=[pl.BlockSpec((tm, LANES), lambda i: (i, 0)),
                   pl.BlockSpec((8, LANES), lambda i: (_part_of_tile(i, bounds), 0))],
        out_shape=[jax.ShapeDtypeStruct((rows, LANES), F32),
                   jax.ShapeDtypeStruct((8 * len(bounds), LANES), F32)],
        scratch_shapes=[pltpu.VMEM((8, LANES), F32)],
        compiler_params=_cparams("arbitrary"),
        name="moe_router",
    )(h, w_r)


def _cast_on_expert_change(te_ref, pairs):
    i = pl.program_id(0)
    prev = te_ref[jnp.maximum(i - 1, 0)]

    @pl.when(jnp.logical_or(i == 0, te_ref[i] != prev))
    def _():
        for w_ref, wb_ref in pairs:
            wb_ref[...] = w_ref[0, 0].astype(wb_ref.dtype)


def _expert_up_kernel(te_ref, tv_ref, x_ref, wg_ref, wu_ref, o_ref, wgb_ref, wub_ref):
    i = pl.program_id(0)
    _cast_on_expert_change(te_ref, ((wg_ref, wgb_ref), (wu_ref, wub_ref)))

    @pl.when(tv_ref[i] > 0)
    def _():
        half = x_ref.shape[1]
        lo, hi = _unpack_pairs(x_ref[...])
        lo = lo.astype(BF16)
        hi = hi.astype(BF16)
        hg = jnp.dot(lo, wgb_ref[:half, :], preferred_element_type=F32)
        hg = hg + jnp.dot(hi, wgb_ref[half:, :], preferred_element_type=F32)
        hu = jnp.dot(lo, wub_ref[:half, :], preferred_element_type=F32)
        hu = hu + jnp.dot(hi, wub_ref[half:, :], preferred_element_type=F32)
        o_ref[...] = ((hg * _sigmoid(hg)) * hu).astype(o_ref.dtype)

    @pl.when(tv_ref[i] == 0)
    def _():
        o_ref[...] = jnp.zeros(o_ref.shape, o_ref.dtype)


def _expert_down_kernel(te_ref, tv_ref, a_ref, wd_ref, o_ref, wdb_ref):
    i = pl.program_id(0)
    _cast_on_expert_change(te_ref, ((wd_ref, wdb_ref),))

    @pl.when(tv_ref[i] > 0)
    def _():
        o_ref[...] = _pack_pairs(jnp.dot(a_ref[...], wdb_ref[...], preferred_element_type=F32))

    @pl.when(tv_ref[i] == 0)
    def _():
        o_ref[...] = jnp.zeros(o_ref.shape, o_ref.dtype)


def _expert_ffn(xs, tile_e, tile_valid, w_g, w_u, w_d, li, tm):
    p, half = xs.shape
    d = 2 * half
    f = w_g.shape[3]

    def wmap(i, te, tv):
        return (li, te[i], 0, 0)

    def rmap(i, te, tv):
        return (i, 0)

    act = pl.pallas_call(
        _expert_up_kernel,
        grid_spec=pltpu.PrefetchScalarGridSpec(
            num_scalar_prefetch=2,
            grid=(p // tm,),
            in_specs=[pl.BlockSpec((tm, half), rmap),
                      pl.BlockSpec((1, 1, d, f), wmap),
                      pl.BlockSpec((1, 1, d, f), wmap)],
            out_specs=pl.BlockSpec((tm, f), rmap),
            scratch_shapes=[pltpu.VMEM((d, f), BF16), pltpu.VMEM((d, f), BF16)]),
        out_shape=jax.ShapeDtypeStruct((p, f), BF16),
        compiler_params=_cparams("arbitrary"),
        name="moe_up",
    )(tile_e, tile_valid, xs, w_g, w_u)
    return pl.pallas_call(
        _expert_down_kernel,
        grid_spec=pltpu.PrefetchScalarGridSpec(
            num_scalar_prefetch=2,
            grid=(p // tm,),
            in_specs=[pl.BlockSpec((tm, f), rmap),
                      pl.BlockSpec((1, 1, f, d), wmap)],
            out_specs=pl.BlockSpec((tm, half), rmap),
            scratch_shapes=[pltpu.VMEM((f, d), BF16)]),
        out_shape=jax.ShapeDtypeStruct((p, half), jnp.uint32),
        compiler_params=_cparams("arbitrary"),
        name="moe_down",
    )(tile_e, tile_valid, act, w_d)


def _combine_kernel(n_lat, tm, tile0, z_ref, ya_ref, yb_ref, info_ref, gl_ref, gc_ref, o_ref):
    is_lat = (pl.program_id(0) + tile0) * tm < n_lat
    gate = jnp.where(is_lat, gl_ref[0], gc_ref[0])
    w1 = info_ref[:, R_WT:R_WT + 1]
    w2 = info_ref[:, R_WT + 1:R_WT + 2]
    half = ya_ref.shape[1]
    a_lo, a_hi = _unpack_pairs(ya_ref[...])
    b_lo, b_hi = _unpack_pairs(yb_ref[...])
    o_ref[:, :half] = z_ref[:, :half] + gate[:, :half] * (w1 * a_lo + w2 * b_lo)
    o_ref[:, half:] = z_ref[:, half:] + gate[:, half:] * (w1 * a_hi + w2 * b_hi)


def _combine(z, y2, info, mods, gate_idx, row0, part_rows, n_lat):
    d = z.shape[1]
    tm = MOE_TILE
    assert row0 % tm == 0 and part_rows % tm == 0 and n_lat % tm == 0
    nt = part_rows // tm
    t0 = row0 // tm
    return pl.pallas_call(
        functools.partial(_combine_kernel, n_lat, tm, t0),
        grid=(nt,),
        in_specs=[pl.BlockSpec((tm, d), lambda i: (i + t0, 0)),
                  pl.BlockSpec((tm, d // 2), lambda i: (i, 0)),
                  pl.BlockSpec((tm, d // 2), lambda i: (i + nt, 0)),
                  pl.BlockSpec((tm, LANES), lambda i: (i + t0, 0)),
                  pl.BlockSpec((1, 1, d), lambda i: (gate_idx, 0, 0)),
                  pl.BlockSpec((1, 1, d), lambda i: (N_MOD + gate_idx, 0, 0))],
        out_specs=pl.BlockSpec((tm, d), lambda i: (i + t0, 0)),
        out_shape=jax.ShapeDtypeStruct(z.shape, F32),
        input_output_aliases={0: 0},
        compiler_params=_cparams("parallel"),
        name="moe_combine",
    )(z, y2, y2, info, mods, mods)


def _rows_of(x, idx):
    return x.at[idx].get(mode="promise_in_bounds", unique_indices=False)


def _moe(h2, h2_pairs, z, mods, w_r, w_g, w_u, w_d, li, n_groups, per_group, rows, n_lat):
    n_exp = n_groups * per_group
    tm = MOE_TILE
    first = (rows // tm // MOE_PARTS) * tm
    part_starts = tuple(range(0, first * MOE_PARTS, first))
    info, cnt = _router(h2, w_r, n_groups, per_group, rows, part_starts)
    experts = jnp.arange(n_exp, dtype=jnp.int32)
    for part, row0 in enumerate(part_starts):
        part_rows = (part_starts[part + 1] if part + 1 < len(part_starts) else rows) - row0
        eid = info[row0:row0 + part_rows, R_EID:R_EID + EXPERT_TOPK].astype(jnp.int32)
        rank = info[row0:row0 + part_rows, R_RANK:R_RANK + EXPERT_TOPK].astype(jnp.int32)
        counts = cnt[8 * part, n_groups:n_groups + n_exp].astype(jnp.int32)
        padded = ((counts + tm - 1) // tm) * tm
        ends = jnp.sum(jnp.where(experts[None, :] <= experts[:, None], padded[None, :], 0), axis=1)
        starts = ends - padded
        pos = jnp.sum(jnp.where(eid[:, :, None] == experts[None, None, :], starts[None, None, :], 0),
                      axis=2) + rank
        pos = pos.T.reshape(-1)
        n_pairs = part_rows * EXPERT_TOPK
        n_tiles = (n_pairs + n_exp * (tm - 1)) // tm + 1
        p = n_tiles * tm
        pair_token = row0 + jnp.arange(n_pairs, dtype=jnp.int32) % part_rows
        filler = row0 + jnp.arange(p, dtype=jnp.int32) % part_rows
        row_token = filler.at[pos].set(pair_token, mode="promise_in_bounds", unique_indices=True)
        tile_start = jnp.arange(n_tiles, dtype=jnp.int32) * tm
        tile_e = jnp.minimum(jnp.sum((ends[None, :] <= tile_start[:, None]).astype(jnp.int32), axis=1),
                             n_exp - 1)
        tile_valid = (tile_start < ends[-1]).astype(jnp.int32)
        xs = _rows_of(h2_pairs, row_token)
        ys = _expert_ffn(xs, tile_e, tile_valid, w_g, w_u, w_d, li, tm)
        z = _combine(z, _rows_of(ys, pos), info, mods, 5, row0, part_rows, n_lat)
    return z


def _rope_tables(n_lat, n_ctx):
    t = jnp.arange(n_lat)
    row = (t // GRID_W).astype(F32)
    col = (t % GRID_W).astype(F32)
    inv = ROPE_THETA ** (-jnp.arange(ROPE_PAIRS, dtype=F32) / ROPE_PAIRS)
    ar = row[:, None] * inv
    ac = col[:, None] * inv
    cos = jnp.concatenate([jnp.cos(ar), jnp.cos(ar), jnp.cos(ac), jnp.cos(ac)], axis=1)
    sin = jnp.concatenate([-jnp.sin(ar), jnp.sin(ar), -jnp.sin(ac), jnp.sin(ac)], axis=1)
    cos = jnp.concatenate([cos, jnp.ones((n_ctx, HEAD_DIM), F32)], axis=0)
    sin = jnp.concatenate([sin, jnp.zeros((n_ctx, HEAD_DIM), F32)], axis=0)
    return cos, sin


def _with_ctx(o_lat, o_ctx, n_lat):
    return lax.dynamic_update_slice(o_lat, o_ctx, (n_lat, 0))


def kernel(x, c, ctx, c_ctx, ada_down, ada_up, ada_bias, norm1_g, norm2_g, w_in_ab, w_out_ab, qnorm_a,
           knorm_a, sink_b, w_in_cd, w_out_cd, rpb_c, dw_w, dw_b, ln_d_g, ln_d_b, router_group,
           router_expert, w_gate, w_up, w_down, final_g):
    assert x.shape[0] == 1 and ctx.shape[0] == 1
    n_lat, d = x.shape[1], x.shape[2]
    n_ctx = ctx.shape[1]
    n_all = n_lat + n_ctx
    depth = ada_down.shape[0]
    assert n_lat % n_ctx == 0 and n_ctx % 128 == 0
    n_groups, per_group = router_expert.shape[1], router_expert.shape[3]
    n_mix = d // HEAD_DIM
    a_w = (n_mix // 2) * HEAD_DIM
    akv_w = (n_mix // 8) * HEAD_DIM
    c_w = (n_mix // 2) * HEAD_DIM
    d_ch = d // 2
    qscale = HEAD_DIM ** -0.5 * LOG2E

    z = jnp.concatenate([x[0], ctx[0]], axis=0)
    cond = jnp.zeros((8, d), F32).at[0].set(c[0]).at[1].set(c_ctx)
    mods_all = _modulation_all(cond, ada_down, ada_up, ada_bias)
    tabs = _rope_tables(n_lat, n_ctx)

    for layer in range(depth):
        need_ctx = layer < depth - 1
        rows = n_all if need_ctx else n_lat
        mods = mods_all[layer, :2].reshape(2 * N_MOD, 1, d)
        h = _norm_mod(z, norm1_g[layer], mods, 0, 1, n_all, n_lat, BF16)
        i = layer // 2
        if layer % 2 == 0:
            w = w_in_ab
            sink = sink_b[i] * LOG2E
            col = 0
            qa = _proj_in(h, w, i, col, a_w, n_all, gain=qnorm_a[i], rope_tabs=tabs, scale=qscale)
            col += a_w
            ka = _proj_in(h, w, i, col, akv_w, n_all, gain=knorm_a[i], rope_tabs=tabs)
            col += akv_w
            va = _proj_in(h, w, i, col, akv_w, n_all, ones_ext=True)
            col += akv_w
            qb = _proj_in(h, w, i, col, a_w, n_all, rope_tabs=tabs, scale=qscale)
            col += a_w
            kb = _proj_in(h, w, i, col, akv_w, n_all, rope_tabs=tabs)
            col += akv_w
            vb = _proj_in(h, w, i, col, akv_w, n_all, ones_ext=True)
            o1 = _global_attn(qa, ka, va, n_lat, n_ctx, rows)
            o2 = _window_attn(qb, kb, vb, sink, n_lat, n_ctx, rows)
            if need_ctx:
                o1 = _with_ctx(o1, _ctx_attn(qa, ka, va, None, n_lat, n_ctx), n_lat)
                o2 = _with_ctx(o2, _ctx_attn(qb, kb, vb, sink, n_lat, n_ctx), n_lat)
            w_out = w_out_ab
        else:
            w = w_in_cd
            qc = _proj_in(h, w, i, 0, c_w, n_all, scale=qscale)
            kc = _proj_in(h, w, i, c_w, c_w, n_all)
            vc = _proj_in(h, w, i, 2 * c_w, c_w, n_all, ones_ext=True)
            u = _proj_glu(h, w, i, 3 * c_w, d_ch, n_all)
            bias = _na_bias_table(rpb_c[i], n_lat // GRID_W)
            o1 = _na_attn(qc, kc, vc, bias, n_lat, n_ctx, rows)
            if need_ctx:
                o1 = _with_ctx(o1, _ctx_attn(qc, kc, vc, None, n_lat, n_ctx), n_lat)
            o2 = _conformer_conv(u, dw_w[i], dw_b[i], ln_d_g[i], ln_d_b[i], rows, n_lat)
            w_out = w_out_cd
        z = _proj_out(o1, o2, w_out, i, z, mods, 2, rows, n_lat)
        h2, h2_pairs = _norm_mod(z, norm2_g[layer], mods, 3, 4, rows, n_lat, BF16, packed=True)
        w_r = jnp.concatenate(
            [router_group[layer], router_expert[layer].transpose(1, 0, 2).reshape(d, n_groups * per_group)],
            axis=1)
        w_r = jnp.pad(w_r, ((0, 0), (0, LANES - w_r.shape[1]))).astype(BF16)
        z = _moe(h2, h2_pairs, z, mods, w_r, w_gate, w_up, w_down, layer, n_groups, per_group, rows, n_lat)
    out = _norm_mod(z, final_g, None, 0, 0, n_lat, n_lat, F32)
    return out[None]
```

```python
import functools
import math

import numpy as np
import jax
import jax.numpy as jnp
from jax import lax
from jax.experimental import pallas as pl
from jax.experimental.pallas import tpu as pltpu

HEAD_DIM = 128
GRID_W = 64
ROPE_PAIRS = HEAD_DIM // 4
ROPE_THETA = 10000.0
WINDOW = 128
NA_KH = 8
NA_KW = 16
CONV_W = 31
N_MOD = 6
EXPERT_TOPK = 2
EPS = 1e-6
NEG = -1e30
LOG2E = math.log2(math.e)
LANES = 128
VMEM_LIMIT = 56 * 1024 * 1024
NA_ROWS = 8
CONV_HALO = 16
EXT = 2 * HEAD_DIM
MOE_TILE = 256
MOE_PARTS = 1

BF16 = jnp.bfloat16
F32 = jnp.float32


def _cparams(*sem):
    return pltpu.CompilerParams(dimension_semantics=sem, vmem_limit_bytes=VMEM_LIMIT)


def _row_tile(m, cap):
    best = None
    for t in range(LANES, min(m, cap) + 1, LANES):
        if m % t == 0:
            best = t
    assert best is not None, (m, cap)
    return best


def _sigmoid(x):
    return 1.0 / (1.0 + jnp.exp(-x))


def _mod_down_kernel(cond_ref, down_ref, o_ref):
    cnd = cond_ref[...]
    a = cnd * _sigmoid(cnd)
    o_ref[0] = jnp.dot(a, down_ref[0], preferred_element_type=F32, precision=lax.Precision.HIGHEST)


def _mod_up_kernel(t_ref, up_ref, b_ref, o_ref):
    o_ref[0] = jnp.dot(t_ref[0], up_ref[0], preferred_element_type=F32,
                       precision=lax.Precision.HIGHEST) + b_ref[0]


def _modulation_all(cond, ada_down, ada_up, ada_bias):
    depth, d, r = ada_down.shape
    n = ada_up.shape[2]
    t = pl.pallas_call(
        _mod_down_kernel,
        grid=(depth,),
        in_specs=[pl.BlockSpec((8, d), lambda l: (0, 0)),
                  pl.BlockSpec((1, d, r), lambda l: (l, 0, 0))],
        out_specs=pl.BlockSpec((1, 8, r), lambda l: (l, 0, 0)),
        out_shape=jax.ShapeDtypeStruct((depth, 8, r), F32),
        compiler_params=_cparams("arbitrary"),
        name="mod_down",
    )(cond, ada_down)
    tn = _row_tile(n, 2048)
    return pl.pallas_call(
        _mod_up_kernel,
        grid=(depth, n // tn),
        in_specs=[pl.BlockSpec((1, 8, r), lambda l, j: (l, 0, 0)),
                  pl.BlockSpec((1, r, tn), lambda l, j: (l, 0, j)),
                  pl.BlockSpec((1, 1, tn), lambda l, j: (l, 0, j))],
        out_specs=pl.BlockSpec((1, 8, tn), lambda l, j: (l, 0, j)),
        out_shape=jax.ShapeDtypeStruct((depth, 8, n), F32),
        compiler_params=_cparams("arbitrary", "arbitrary"),
        name="mod_up",
    )(t, ada_up, ada_bias.reshape(depth, 1, n))


def _pack_pairs(x):
    half = x.shape[1] // 2
    lo = lax.bitcast_convert_type(x[:, :half].astype(BF16).astype(F32), jnp.uint32)
    hi = lax.bitcast_convert_type(x[:, half:].astype(BF16).astype(F32), jnp.uint32)
    return lax.shift_right_logical(lo, jnp.uint32(16)) | (hi & jnp.uint32(0xFFFF0000))


def _unpack_pairs(w):
    lo = lax.bitcast_convert_type(lax.shift_left(w, jnp.uint32(16)), F32)
    hi = lax.bitcast_convert_type(w & jnp.uint32(0xFFFF0000), F32)
    return lo, hi


def _norm_mod_kernel(n_lat, tm, modulated, packed, z_ref, g_ref, *rest):
    rest = list(rest)
    p_ref = rest.pop() if packed else None
    if modulated:
        sh_l, sc_l, sh_c, sc_c, o_ref = rest
    else:
        (o_ref,) = rest
    x = z_ref[...]
    r = lax.rsqrt(jnp.mean(x * x, axis=-1, keepdims=True) + EPS)
    if modulated:
        is_lat = pl.program_id(0) * tm < n_lat
        shift = jnp.where(is_lat, sh_l[0], sh_c[0])
        scale = jnp.where(is_lat, sc_l[0], sc_c[0])
        h = (x * r) * (g_ref[...] * (1.0 + scale)) + shift
    else:
        h = x * r * g_ref[...]
    o_ref[...] = h.astype(o_ref.dtype)
    if packed:
        p_ref[...] = _pack_pairs(h)


def _norm_mod(z, g, mods, a, b, rows, n_lat, out_dtype, packed=False):
    d = z.shape[1]
    tm = _row_tile(math.gcd(rows, n_lat), 256)
    modulated = mods is not None
    in_specs = [pl.BlockSpec((tm, d), lambda i: (i, 0)),
                pl.BlockSpec((1, d), lambda i: (0, 0))]
    args = [z, g.reshape(1, d)]
    if modulated:
        for idx in (a, b, N_MOD + a, N_MOD + b):
            in_specs.append(pl.BlockSpec((1, 1, d), lambda i, idx=idx: (idx, 0, 0)))
            args.append(mods)
    out_specs = pl.BlockSpec((tm, d), lambda i: (i, 0))
    out_shape = jax.ShapeDtypeStruct((rows, d), out_dtype)
    if packed:
        out_specs = [out_specs, pl.BlockSpec((tm, d // 2), lambda i: (i, 0))]
        out_shape = [out_shape, jax.ShapeDtypeStruct((rows, d // 2), jnp.uint32)]
    return pl.pallas_call(
        functools.partial(_norm_mod_kernel, n_lat, tm, modulated, packed),
        grid=(rows // tm,),
        in_specs=in_specs,
        out_specs=out_specs,
        out_shape=out_shape,
        compiler_params=_cparams("parallel"),
        name="norm_mod",
    )(*args)


def _rope_swap(x):
    lane = lax.broadcasted_iota(jnp.int32, x.shape, 1)
    first_half = (lane % (2 * ROPE_PAIRS)) < ROPE_PAIRS
    return jnp.where(first_half, pltpu.roll(x, HEAD_DIM - ROPE_PAIRS, 1), pltpu.roll(x, ROPE_PAIRS, 1))


def _cast_weight_tile(w_ref, wb_ref):
    @pl.when(pl.program_id(1) == 0)
    def _():
        wb_ref[...] = w_ref[0].astype(wb_ref.dtype)


def _proj_in_kernel(norm, rope, scale, ones_ext, heads, a_ref, w_ref, *rest):
    rest = list(rest)
    g_ref = rest.pop(0) if norm else None
    if rope:
        c_ref = rest.pop(0)
        s_ref = rest.pop(0)
    o_ref, wb_ref = rest
    _cast_weight_tile(w_ref, wb_ref)
    acc = jnp.dot(a_ref[...], wb_ref[...], preferred_element_type=F32)
    if not (norm or rope or ones_ext):
        if scale != 1.0:
            acc = acc * scale
        o_ref[...] = acc.astype(o_ref.dtype)
        return
    ow = EXT if ones_ext else HEAD_DIM
    for hh in range(heads):
        x = acc[:, hh * HEAD_DIM:(hh + 1) * HEAD_DIM]
        if norm:
            r = lax.rsqrt(jnp.mean(x * x, axis=-1, keepdims=True) + EPS)
            x = x * r * g_ref[...]
        if rope:
            x = x * c_ref[...] + _rope_swap(x) * s_ref[...]
        if scale != 1.0:
            x = x * scale
        o_ref[:, hh * ow:hh * ow + HEAD_DIM] = x.astype(o_ref.dtype)
        if ones_ext:
            o_ref[:, hh * ow + HEAD_DIM:(hh + 1) * ow] = jnp.ones(x.shape, o_ref.dtype)


def _proj_in(h, w, li, col0, width, rows, *, gain=None, rope_tabs=None, scale=1.0, ones_ext=False):
    k = h.shape[1]
    tm = _row_tile(rows, 1280)
    tn = math.gcd(math.gcd(width, 512), col0) if col0 else math.gcd(width, 512)
    norm = gain is not None
    rope = rope_tabs is not None
    in_specs = [pl.BlockSpec((tm, k), lambda j, i: (i, 0)),
                pl.BlockSpec((1, k, tn), lambda j, i: (li, 0, col0 // tn + j))]
    args = [h, w]
    if norm:
        in_specs.append(pl.BlockSpec((1, HEAD_DIM), lambda j, i: (0, 0)))
        args.append(gain.reshape(1, HEAD_DIM))
    if rope:
        for tab in rope_tabs:
            in_specs.append(pl.BlockSpec((tm, HEAD_DIM), lambda j, i: (i, 0)))
            args.append(tab)
    mult = 2 if ones_ext else 1
    return pl.pallas_call(
        functools.partial(_proj_in_kernel, norm, rope, scale, ones_ext, tn // HEAD_DIM),
        grid=(width // tn, rows // tm),
        in_specs=in_specs,
        out_specs=pl.BlockSpec((tm, mult * tn), lambda j, i: (i, j)),
        out_shape=jax.ShapeDtypeStruct((rows, mult * width), BF16),
        scratch_shapes=[pltpu.VMEM((k, tn), BF16)],
        compiler_params=_cparams("parallel", "arbitrary"),
        name="proj_in",
    )(*args)


def _proj_glu_kernel(a_ref, wa_ref, wg_ref, o_ref, wab_ref, wgb_ref):
    _cast_weight_tile(wa_ref, wab_ref)
    _cast_weight_tile(wg_ref, wgb_ref)
    a = a_ref[...]
    va = jnp.dot(a, wab_ref[...], preferred_element_type=F32)
    vg = jnp.dot(a, wgb_ref[...], preferred_element_type=F32)
    o_ref[...] = (va * _sigmoid(vg)).astype(o_ref.dtype)


def _proj_glu(h, w, li, col0, width, rows):
    k = h.shape[1]
    tm = _row_tile(rows, 1280)
    tn = math.gcd(math.gcd(width, 256), col0)
    return pl.pallas_call(
        _proj_glu_kernel,
        grid=(width // tn, rows // tm),
        in_specs=[pl.BlockSpec((tm, k), lambda j, i: (i, 0)),
                  pl.BlockSpec((1, k, tn), lambda j, i: (li, 0, col0 // tn + j)),
                  pl.BlockSpec((1, k, tn), lambda j, i: (li, 0, (col0 + width) // tn + j))],
        out_specs=pl.BlockSpec((tm, tn), lambda j, i: (i, j)),
        out_shape=jax.ShapeDtypeStruct((rows, width), F32),
        scratch_shapes=[pltpu.VMEM((k, tn), BF16), pltpu.VMEM((k, tn), BF16)],
        compiler_params=_cparams("parallel", "arbitrary"),
        name="proj_glu",
    )(h, w, w)


def _proj_out_kernel(n_lat, tm, a1_ref, a2_ref, w1_ref, w2_ref, z_ref, gl_ref, gc_ref, o_ref,
                     w1b_ref, w2b_ref):
    _cast_weight_tile(w1_ref, w1b_ref)
    _cast_weight_tile(w2_ref, w2b_ref)
    acc = jnp.dot(a1_ref[...], w1b_ref[...], preferred_element_type=F32)
    acc = acc + jnp.dot(a2_ref[...], w2b_ref[...], preferred_element_type=F32)
    rows = pl.program_id(1) * tm + lax.broadcasted_iota(jnp.int32, (tm, 1), 0)
    gate = jnp.where(rows < n_lat, gl_ref[0], gc_ref[0])
    o_ref[...] = z_ref[...] + gate * acc


def _proj_out(a1, a2, w, li, z, mods, gate_idx, rows, n_lat):
    k1, k2 = a1.shape[1], a2.shape[1]
    assert k1 == k2
    n = w.shape[2]
    tm = _row_tile(rows, 1280)
    tn = math.gcd(n, 512)
    return pl.pallas_call(
        functools.partial(_proj_out_kernel, n_lat, tm),
        grid=(n // tn, rows // tm),
        in_specs=[pl.BlockSpec((tm, k1), lambda j, i: (i, 0)),
                  pl.BlockSpec((tm, k2), lambda j, i: (i, 0)),
                  pl.BlockSpec((1, k1, tn), lambda j, i: (li, 0, j)),
                  pl.BlockSpec((1, k2, tn), lambda j, i: (li, 1, j)),
                  pl.BlockSpec((tm, tn), lambda j, i: (i, j)),
                  pl.BlockSpec((1, 1, tn), lambda j, i: (gate_idx, 0, j)),
                  pl.BlockSpec((1, 1, tn), lambda j, i: (N_MOD + gate_idx, 0, j))],
        out_specs=pl.BlockSpec((tm, tn), lambda j, i: (i, j)),
        out_shape=jax.ShapeDtypeStruct((rows, n), F32),
        scratch_shapes=[pltpu.VMEM((k1, tn), BF16), pltpu.VMEM((k2, tn), BF16)],
        compiler_params=_cparams("parallel", "arbitrary"),
        name="proj_out",
    )(a1, a2, w, w, z, mods, mods)


def _nt_dot(a, b):
    return lax.dot_general(a, b, (((1,), (1,)), ((), ())), preferred_element_type=F32)


def _softmax_pv(s, v_ext, m_ref, acc_ref):
    m_prev = m_ref[...]
    m_new = jnp.maximum(m_prev, jnp.max(s, axis=1, keepdims=True))
    alpha = jnp.exp2(m_prev - m_new)
    p = jnp.exp2(s - jnp.tile(m_new, (1, s.shape[1] // LANES)))
    pv = jnp.dot(p.astype(v_ext.dtype), v_ext, preferred_element_type=F32)
    acc_ref[...] = jnp.tile(alpha, (1, EXT // LANES)) * acc_ref[...] + pv
    m_ref[...] = m_new


def _softmax_chunk(q, k, v_ext, m_ref, acc_ref, mask=None):
    s = _nt_dot(q, k)
    if mask is not None:
        s = jnp.where(mask, s, NEG)
    _softmax_pv(s, v_ext, m_ref, acc_ref)


def _stack_heads(groups, tq, q_ref, qs_ref):
    for g in range(groups):
        qs_ref[g * tq:(g + 1) * tq, :] = q_ref[:, g * HEAD_DIM:(g + 1) * HEAD_DIM]


def _write_heads(groups, tq, acc_ref, o_ref):
    for g in range(groups):
        a = acc_ref[g * tq:(g + 1) * tq, :]
        o_ref[:, g * HEAD_DIM:(g + 1) * HEAD_DIM] = (a[:, :HEAD_DIM] / a[:, HEAD_DIM:]).astype(o_ref.dtype)


def _global_attn_kernel(groups, tq, kc, n_chunks, q_ref, k_ref, v_ref, kc_ref, vc_ref, o_ref,
                        qs_ref, m_ref, acc_ref, s0_ref, s1_ref):
    _stack_heads(groups, tq, q_ref, qs_ref)
    m_ref[...] = jnp.full(m_ref.shape, NEG, F32)
    acc_ref[...] = jnp.zeros(acc_ref.shape, F32)
    _softmax_chunk(qs_ref[...], kc_ref[...], vc_ref[...], m_ref, acc_ref)

    def logits(c):
        return _nt_dot(qs_ref[...], k_ref[pl.ds(pl.multiple_of(c * kc, kc), kc), :])

    def update(s_ref, c):
        _softmax_pv(s_ref[...], v_ref[pl.ds(pl.multiple_of(c * kc, kc), kc), :], m_ref, acc_ref)

    s0_ref[...] = logits(0)

    def body(t, carry):
        c = 2 * t
        s1_ref[...] = logits(c + 1)
        update(s0_ref, c)
        s0_ref[...] = logits(c + 2)
        update(s1_ref, c + 1)
        return carry

    lax.fori_loop(0, n_chunks // 2 - 1, body, 0)
    s1_ref[...] = logits(n_chunks - 1)
    update(s0_ref, n_chunks - 2)
    update(s1_ref, n_chunks - 1)
    _write_heads(groups, tq, acc_ref, o_ref)


def _global_attn(q, k, v_ext, n_lat, n_ctx, out_rows):
    hq = q.shape[1] // HEAD_DIM
    hkv = k.shape[1] // HEAD_DIM
    groups = hq // hkv
    tq = _row_tile(n_lat, 256)
    kc = _row_tile(n_lat, 1024)
    n_chunks = n_lat // kc
    assert n_chunks >= 2 and n_chunks % 2 == 0
    gw = groups * HEAD_DIM
    ctx_blk = n_lat // n_ctx
    rows = groups * tq
    return pl.pallas_call(
        functools.partial(_global_attn_kernel, groups, tq, kc, n_chunks),
        grid=(hkv, n_lat // tq),
        in_specs=[pl.BlockSpec((tq, gw), lambda h, i: (i, h)),
                  pl.BlockSpec((n_lat, HEAD_DIM), lambda h, i: (0, h)),
                  pl.BlockSpec((n_lat, EXT), lambda h, i: (0, h)),
                  pl.BlockSpec((n_ctx, HEAD_DIM), lambda h, i: (ctx_blk, h)),
                  pl.BlockSpec((n_ctx, EXT), lambda h, i: (ctx_blk, h))],
        out_specs=pl.BlockSpec((tq, gw), lambda h, i: (i, h)),
        out_shape=jax.ShapeDtypeStruct((out_rows, q.shape[1]), BF16),
        scratch_shapes=[pltpu.VMEM((rows, HEAD_DIM), BF16),
                        pltpu.VMEM((rows, LANES), F32),
                        pltpu.VMEM((rows, EXT), F32),
                        pltpu.VMEM((rows, kc), F32),
                        pltpu.VMEM((rows, kc), F32)],
        compiler_params=_cparams("parallel", "arbitrary"),
        name="global_attn",
    )(q, k, v_ext, k, v_ext)


def _window_attn_kernel(groups, tq, span, n_lat, sink_ref, q_ref, k_ref, v_ref, kc_ref, vc_ref, o_ref,
                        qs_ref, m_ref, acc_ref):
    h = pl.program_id(0)
    i = pl.program_id(1)
    rows = groups * tq
    _stack_heads(groups, tq, q_ref, qs_ref)
    for g in range(groups):
        m_ref[g * tq:(g + 1) * tq, :] = jnp.full((tq, LANES), sink_ref[h * groups + g], F32)
    lane = lax.broadcasted_iota(jnp.int32, (rows, EXT), 1)
    acc_ref[...] = jnp.where(lane >= HEAD_DIM, 1.0, 0.0).astype(F32)
    _softmax_chunk(qs_ref[...], kc_ref[...], vc_ref[...], m_ref, acc_ref)
    start = pl.multiple_of(jnp.clip(i * tq - WINDOW, 0, n_lat - span), WINDOW)
    qpos = i * tq + lax.broadcasted_iota(jnp.int32, (tq, span), 0)
    kpos = start + lax.broadcasted_iota(jnp.int32, (tq, span), 1)
    band = jnp.where(jnp.abs(qpos - kpos) <= WINDOW, 0.0, NEG)
    s = _nt_dot(qs_ref[...], k_ref[pl.ds(start, span), :])
    s = (s.reshape(groups, tq, span) + band[None]).reshape(rows, span)
    _softmax_pv(s, v_ref[pl.ds(start, span), :], m_ref, acc_ref)
    _write_heads(groups, tq, acc_ref, o_ref)


def _window_attn(q, k, v_ext, sink, n_lat, n_ctx, out_rows):
    hq = q.shape[1] // HEAD_DIM
    hkv = k.shape[1] // HEAD_DIM
    groups = hq // hkv
    tq = _row_tile(n_lat, 256)
    span = tq + 2 * WINDOW
    assert tq % WINDOW == 0 and n_lat >= span
    gw = groups * HEAD_DIM
    ctx_blk = n_lat // n_ctx
    rows = groups * tq
    return pl.pallas_call(
        functools.partial(_window_attn_kernel, groups, tq, span, n_lat),
        grid=(hkv, n_lat // tq),
        in_specs=[pl.BlockSpec(memory_space=pltpu.SMEM),
                  pl.BlockSpec((tq, gw), lambda h, i: (i, h)),
                  pl.BlockSpec((n_lat, HEAD_DIM), lambda h, i: (0, h)),
                  pl.BlockSpec((n_lat, EXT), lambda h, i: (0, h)),
                  pl.BlockSpec((n_ctx, HEAD_DIM), lambda h, i: (ctx_blk, h)),
                  pl.BlockSpec((n_ctx, EXT), lambda h, i: (ctx_blk, h))],
        out_specs=pl.BlockSpec((tq, gw), lambda h, i: (i, h)),
        out_shape=jax.ShapeDtypeStruct((out_rows, q.shape[1]), BF16),
        scratch_shapes=[pltpu.VMEM((rows, HEAD_DIM), BF16),
                        pltpu.VMEM((rows, LANES), F32),
                        pltpu.VMEM((rows, EXT), F32)],
        compiler_params=_cparams("parallel", "arbitrary"),
        name="window_attn",
    )(sink, q, k, v_ext, k, v_ext)


def _ctx_attn_kernel(groups, has_sink, *refs):
    if has_sink:
        sink_ref, q_ref, k_ref, v_ref, o_ref = refs
    else:
        q_ref, k_ref, v_ref, o_ref = refs
    h = pl.program_id(0)
    for g in range(groups):
        q = q_ref[:, g * HEAD_DIM:(g + 1) * HEAD_DIM]
        s = _nt_dot(q, k_ref[...])
        m = jnp.max(s, axis=-1, keepdims=True)
        if has_sink:
            snk = sink_ref[h * groups + g]
            m = jnp.maximum(m, snk)
        p = jnp.exp2(s - m)
        o = jnp.dot(p.astype(v_ref.dtype), v_ref[...], preferred_element_type=F32)
        l = o[:, HEAD_DIM:]
        if has_sink:
            l = l + jnp.exp2(snk - m)
        o_ref[:, g * HEAD_DIM:(g + 1) * HEAD_DIM] = (o[:, :HEAD_DIM] / l).astype(o_ref.dtype)


def _ctx_attn(q, k, v_ext, sink, n_lat, n_ctx):
    hq = q.shape[1] // HEAD_DIM
    hkv = k.shape[1] // HEAD_DIM
    groups = hq // hkv
    gw = groups * HEAD_DIM
    ctx_blk = n_lat // n_ctx
    has_sink = sink is not None
    in_specs = [pl.BlockSpec((n_ctx, gw), lambda h: (ctx_blk, h)),
                pl.BlockSpec((n_ctx, HEAD_DIM), lambda h: (ctx_blk, h)),
                pl.BlockSpec((n_ctx, EXT), lambda h: (ctx_blk, h))]
    args = [q, k, v_ext]
    if has_sink:
        in_specs.insert(0, pl.BlockSpec(memory_space=pltpu.SMEM))
        args.insert(0, sink)
    return pl.pallas_call(
        functools.partial(_ctx_attn_kernel, groups, has_sink),
        grid=(hkv,),
        in_specs=in_specs,
        out_specs=pl.BlockSpec((n_ctx, gw), lambda h: (0, h)),
        out_shape=jax.ShapeDtypeStruct((n_ctx, q.shape[1]), BF16),
        compiler_params=_cparams("parallel"),
        name="ctx_attn",
    )(*args)


def _na_bias_table(rpb, n_rows):
    kh = min(NA_KH, n_rows)
    assert kh == NA_KH and NA_ROWS == NA_KH and n_rows % NA_ROWS == 0 and n_rows // NA_ROWS >= 2
    half = NA_ROWS // 2
    qr = np.arange(NA_ROWS)
    krel = np.where(qr[:, None] < half, np.arange(-half, 0)[None, :], NA_ROWS + np.arange(half)[None, :])
    krel = np.concatenate([krel, np.broadcast_to(np.arange(NA_ROWS), (NA_ROWS, NA_ROWS))], axis=1)
    nk = krel.shape[1]
    ri = np.zeros((3, NA_ROWS, nk), np.int32)
    rvalid = np.zeros((3, NA_ROWS, nk), bool)
    big = 4 * NA_ROWS
    for var, (r0, rows) in enumerate(((0, big), (NA_ROWS, big), (n_rows - NA_ROWS, n_rows))):
        r = r0 + qr[:, None]
        key = r0 + krel
        rs = np.clip(r - kh // 2, 0, rows - kh)
        rvalid[var] = (key >= rs) & (key < rs + kh)
        ri[var] = np.clip(key - r + NA_KH - 1, 0, 2 * NA_KH - 2)
    qc = np.arange(GRID_W)
    cs = np.clip(qc - NA_KW // 2, 0, GRID_W - NA_KW)
    kc = np.arange(GRID_W)
    cvalid = (kc[None, :] >= cs[:, None]) & (kc[None, :] < cs[:, None] + NA_KW)
    ci = np.clip(kc[None, :] - qc[:, None] + NA_KW - 1, 0, 2 * NA_KW - 2)
    t = jnp.take(rpb * LOG2E, jnp.asarray(ci), axis=2)
    t = t[:, jnp.asarray(ri)]
    valid = rvalid[:, :, :, None, None] & cvalid[None, None, None, :, :]
    t = jnp.where(jnp.asarray(valid)[None], t, NEG)
    t = t.transpose(0, 1, 2, 4, 3, 5)
    return t.reshape(rpb.shape[0], 3, NA_ROWS * GRID_W, nk * GRID_W)


def _na_kernel(nq, q_ref, kp_ref, k0_ref, kn_ref, vp_ref, v0_ref, vn_ref, kc_ref, vc_ref, b_ref, o_ref):
    half = nq // 2
    q = q_ref[...]
    s_nb = jnp.concatenate([_nt_dot(q[:half], kp_ref[...]), _nt_dot(q[half:], kn_ref[...])], axis=0)
    s_nb = s_nb + b_ref[0, 0, :, :half]
    s_own = _nt_dot(q, k0_ref[...]) + b_ref[0, 0, :, half:]
    s_ctx = _nt_dot(q, kc_ref[...])
    m = jnp.maximum(jnp.maximum(s_nb.max(axis=-1, keepdims=True), s_own.max(axis=-1, keepdims=True)),
                    s_ctx.max(axis=-1, keepdims=True))
    p_nb = jnp.exp2(s_nb - m).astype(vp_ref.dtype)
    p_own = jnp.exp2(s_own - m).astype(v0_ref.dtype)
    p_ctx = jnp.exp2(s_ctx - m).astype(vc_ref.dtype)
    o = jnp.concatenate([jnp.dot(p_nb[:half], vp_ref[...], preferred_element_type=F32),
                         jnp.dot(p_nb[half:], vn_ref[...], preferred_element_type=F32)], axis=0)
    o = o + jnp.dot(p_own, v0_ref[...], preferred_element_type=F32)
    o = o + jnp.dot(p_ctx, vc_ref[...], preferred_element_type=F32)
    o_ref[...] = (o[:, :HEAD_DIM] / o[:, HEAD_DIM:]).astype(o_ref.dtype)


def _na_attn(q, k, v_ext, bias, n_lat, n_ctx, out_rows):
    heads = q.shape[1] // HEAD_DIM
    nq = NA_ROWS * GRID_W
    half = nq // 2
    nb = n_lat // nq
    ctx_blk = n_lat // n_ctx

    def prev_map(h, b):
        return (jnp.maximum(2 * b - 1, 0), h)

    def next_map(h, b):
        return (jnp.minimum(2 * b + 2, 2 * nb - 1), h)

    def own_map(h, b):
        return (b, h)

    def ctx_map(h, b):
        return (ctx_blk, h)

    def bias_map(h, b):
        return (h, jnp.where(b == 0, 0, jnp.where(b == nb - 1, 2, 1)), 0, 0)

    return pl.pallas_call(
        functools.partial(_na_kernel, nq),
        grid=(heads, nb),
        in_specs=[pl.BlockSpec((nq, HEAD_DIM), own_map),
                  pl.BlockSpec((half, HEAD_DIM), prev_map), pl.BlockSpec((nq, HEAD_DIM), own_map),
                  pl.BlockSpec((half, HEAD_DIM), next_map),
                  pl.BlockSpec((half, EXT), prev_map), pl.BlockSpec((nq, EXT), own_map),
                  pl.BlockSpec((half, EXT), next_map),
                  pl.BlockSpec((n_ctx, HEAD_DIM), ctx_map), pl.BlockSpec((n_ctx, EXT), ctx_map),
                  pl.BlockSpec((1, 1, nq, half + nq), bias_map)],
        out_specs=pl.BlockSpec((nq, HEAD_DIM), own_map),
        out_shape=jax.ShapeDtypeStruct((out_rows, q.shape[1]), BF16),
        compiler_params=_cparams("parallel", "arbitrary"),
        name="na_attn",
    )(q, k, k, k, v_ext, v_ext, v_ext, k, v_ext, bias)


def _conv_kernel(tr, n_lat_tiles, n_tiles, ch_chunk, prev_ref, cur_ref, next_ref, w_ref, b_ref,
                 g_ref, beta_ref, o_ref, win_ref, y_ref):
    i = pl.program_id(0)
    c = cur_ref.shape[1]
    prev_ok = jnp.logical_and(i != 0, i != n_lat_tiles)
    next_ok = jnp.logical_and(i != n_lat_tiles - 1, i != n_tiles - 1)
    win_ref[0:CONV_HALO, :] = jnp.where(prev_ok, prev_ref[...], 0.0)
    win_ref[CONV_HALO:CONV_HALO + tr, :] = cur_ref[...]
    win_ref[CONV_HALO + tr:, :] = jnp.where(next_ok, next_ref[...], 0.0)
    off = CONV_HALO - CONV_W // 2
    rc = 64

    def col_body(cc, carry):
        c0 = pl.multiple_of(cc * ch_chunk, ch_chunk)
        for r0 in range(0, tr, rc):
            acc = jnp.zeros((rc, ch_chunk), F32)
            for t in range(CONV_W):
                acc = acc + win_ref[r0 + off + t:r0 + off + t + rc, pl.ds(c0, ch_chunk)] * \
                    w_ref[t:t + 1, pl.ds(c0, ch_chunk)]
            y_ref[r0:r0 + rc, pl.ds(c0, ch_chunk)] = acc + b_ref[:, pl.ds(c0, ch_chunk)]
        return carry

    lax.fori_loop(0, c // ch_chunk, col_body, 0)
    y = y_ref[...]
    mu = jnp.mean(y, axis=-1, keepdims=True)
    var = jnp.mean(jnp.square(y - mu), axis=-1, keepdims=True)
    yn = (y - mu) * lax.rsqrt(var + EPS) * g_ref[...] + beta_ref[...]
    o_ref[...] = (yn * _sigmoid(yn)).astype(o_ref.dtype)


def _conformer_conv(u, dw_w, dw_b, ln_g, ln_b, rows, n_lat):
    c = u.shape[1]
    tr = 256
    assert n_lat % tr == 0 and rows % tr == 0
    hb = tr // CONV_HALO
    n_tiles = rows // tr
    last_halo = rows // CONV_HALO - 1
    ch_chunk = math.gcd(c, 256)
    return pl.pallas_call(
        functools.partial(_conv_kernel, tr, n_lat // tr, n_tiles, ch_chunk),
        grid=(n_tiles,),
        in_specs=[pl.BlockSpec((CONV_HALO, c), lambda i: (jnp.maximum(i * hb - 1, 0), 0)),
                  pl.BlockSpec((tr, c), lambda i: (i, 0)),
                  pl.BlockSpec((CONV_HALO, c), lambda i: (jnp.minimum((i + 1) * hb, last_halo), 0)),
                  pl.BlockSpec((CONV_W, c), lambda i: (0, 0)),
                  pl.BlockSpec((1, c), lambda i: (0, 0)),
                  pl.BlockSpec((1, c), lambda i: (0, 0)),
                  pl.BlockSpec((1, c), lambda i: (0, 0))],
        out_specs=pl.BlockSpec((tr, c), lambda i: (i, 0)),
        out_shape=jax.ShapeDtypeStruct((rows, c), BF16),
        scratch_shapes=[pltpu.VMEM((tr + 2 * CONV_HALO, c), F32),
                        pltpu.VMEM((tr, c), F32)],
        compiler_params=_cparams("parallel"),
        name="conformer_conv",
    )(u, u, u, dw_w, dw_b.reshape(1, c), ln_g.reshape(1, c), ln_b.reshape(1, c))


R_EID, R_WT, R_RANK = 0, 2, 4


def _part_of_tile(i, bounds):
    return sum((i >= b).astype(jnp.int32) for b in bounds[1:]) if len(bounds) > 1 else 0 * i


def _router_kernel(n_groups, per_group, tm, bounds, h_ref, w_ref, info_ref, cnt_ref, carry_ref):
    i = pl.program_id(0)
    starts_part = functools.reduce(jnp.logical_or, [i == b for b in bounds])

    @pl.when(starts_part)
    def _():
        carry_ref[...] = jnp.zeros(carry_ref.shape, F32)

    lg = jnp.dot(h_ref[...], w_ref[...], preferred_element_type=F32)
    lane = lax.broadcasted_iota(jnp.int32, lg.shape, 1).astype(F32)
    is_g = lane < n_groups
    gl = jnp.where(is_g, lg, NEG)
    mg = jnp.max(gl, axis=-1, keepdims=True)
    gsel = jnp.min(jnp.where(gl == mg, lane, float(LANES)), axis=-1, keepdims=True)
    gsum = jnp.sum(jnp.where(is_g, jnp.exp(gl - mg), 0.0), axis=-1, keepdims=True)
    lo = n_groups + gsel * per_group
    el = jnp.where((lane >= lo) & (lane < lo + per_group), lg, NEG)
    v1 = jnp.max(el, axis=-1, keepdims=True)
    i1 = jnp.min(jnp.where(el == v1, lane, float(LANES)), axis=-1, keepdims=True)
    el2 = jnp.where(lane == i1, NEG, el)
    v2 = jnp.max(el2, axis=-1, keepdims=True)
    i2 = jnp.min(jnp.where(el2 == v2, lane, float(LANES)), axis=-1, keepdims=True)
    e2 = jnp.exp(v2 - v1)
    w1 = 1.0 / ((1.0 + e2) * gsum)
    w2 = e2 * w1
    hit1 = lane == i1
    hit2 = lane == i2
    cnt = jnp.where(hit1 | hit2, 1.0, 0.0)
    tri = (lax.broadcasted_iota(jnp.int32, (tm, tm), 1) < lax.broadcasted_iota(jnp.int32, (tm, tm), 0))
    before = jnp.dot(jnp.where(tri, 1.0, 0.0).astype(BF16), cnt.astype(BF16), preferred_element_type=F32)
    tot = carry_ref[0:1, :] + before
    r1 = jnp.sum(jnp.where(hit1, tot, 0.0), axis=-1, keepdims=True)
    r2 = jnp.sum(jnp.where(hit2, tot, 0.0), axis=-1, keepdims=True)
    carry_ref[...] = carry_ref[...] + jnp.sum(cnt, axis=0, keepdims=True)
    info = jnp.zeros(lg.shape, F32)
    for col, val in ((R_EID, i1 - n_groups), (R_EID + 1, i2 - n_groups), (R_WT, w1), (R_WT + 1, w2),
                     (R_RANK, r1), (R_RANK + 1, r2)):
        info = jnp.where(lane == col, val, info)
    info_ref[...] = info
    cnt_ref[...] = carry_ref[...]


def _router(h, w_r, n_groups, per_group, rows, part_starts):
    d = h.shape[1]
    tm = MOE_TILE
    assert all(s % tm == 0 for s in part_starts) and rows % tm == 0
    bounds = tuple(s // tm for s in part_starts)
    return pl.pallas_call(
        functools.partial(_router_kernel, n_groups, per_group, tm, bounds),
        grid=(rows // tm,),
        in_specs=[pl.BlockSpec((tm, d), lambda i: (i, 0)),
                  pl.BlockSpec((d, LANES), lambda i: (0, 0))],
        out_specs=[pl.BlockSpec((tm, LANES), lambda i: (i, 0)),
                   pl.BlockSpec((8, LANES), lambda i: (_part_of_tile(i, bounds), 0))],
        out_shape=[jax.ShapeDtypeStruct((rows, LANES), F32),
                   jax.ShapeDtypeStruct((8 * len(bounds), LANES), F32)],
        scratch_shapes=[pltpu.VMEM((8, LANES), F32)],
        compiler_params=_cparams("arbitrary"),
        name="moe_router",
    )(h, w_r)


def _cast_on_expert_change(te_ref, pairs):
    i = pl.program_id(0)
    prev = te_ref[jnp.maximum(i - 1, 0)]

    @pl.when(jnp.logical_or(i == 0, te_ref[i] != prev))
    def _():
        for w_ref, wb_ref in pairs:
            wb_ref[...] = w_ref[0, 0].astype(wb_ref.dtype)


def _expert_up_kernel(te_ref, tv_ref, x_ref, wg_ref, wu_ref, o_ref, wgb_ref, wub_ref):
    i = pl.program_id(0)
    _cast_on_expert_change(te_ref, ((wg_ref, wgb_ref), (wu_ref, wub_ref)))

    @pl.when(tv_ref[i] > 0)
    def _():
        half = x_ref.shape[1]
        lo, hi = _unpack_pairs(x_ref[...])
        lo = lo.astype(BF16)
        hi = hi.astype(BF16)
        hg = jnp.dot(lo, wgb_ref[:half, :], preferred_element_type=F32)
        hg = hg + jnp.dot(hi, wgb_ref[half:, :], preferred_element_type=F32)
        hu = jnp.dot(lo, wub_ref[:half, :], preferred_element_type=F32)
        hu = hu + jnp.dot(hi, wub_ref[half:, :], preferred_element_type=F32)
        o_ref[...] = ((hg * _sigmoid(hg)) * hu).astype(o_ref.dtype)

    @pl.when(tv_ref[i] == 0)
    def _():
        o_ref[...] = jnp.zeros(o_ref.shape, o_ref.dtype)


def _expert_down_kernel(te_ref, tv_ref, a_ref, wd_ref, o_ref, wdb_ref):
    i = pl.program_id(0)
    _cast_on_expert_change(te_ref, ((wd_ref, wdb_ref),))

    @pl.when(tv_ref[i] > 0)
    def _():
        o_ref[...] = _pack_pairs(jnp.dot(a_ref[...], wdb_ref[...], preferred_element_type=F32))

    @pl.when(tv_ref[i] == 0)
    def _():
        o_ref[...] = jnp.zeros(o_ref.shape, o_ref.dtype)


def _expert_ffn(xs, tile_e, tile_valid, w_g, w_u, w_d, li, tm):
    p, half = xs.shape
    d = 2 * half
    f = w_g.shape[3]

    def wmap(i, te, tv):
        return (li, te[i], 0, 0)

    def rmap(i, te, tv):
        return (i, 0)

    act = pl.pallas_call(
        _expert_up_kernel,
        grid_spec=pltpu.PrefetchScalarGridSpec(
            num_scalar_prefetch=2,
            grid=(p // tm,),
            in_specs=[pl.BlockSpec((tm, half), rmap),
                      pl.BlockSpec((1, 1, d, f), wmap),
                      pl.BlockSpec((1, 1, d, f), wmap)],
            out_specs=pl.BlockSpec((tm, f), rmap),
            scratch_shapes=[pltpu.VMEM((d, f), BF16), pltpu.VMEM((d, f), BF16)]),
        out_shape=jax.ShapeDtypeStruct((p, f), BF16),
        compiler_params=_cparams("arbitrary"),
        name="moe_up",
    )(tile_e, tile_valid, xs, w_g, w_u)
    return pl.pallas_call(
        _expert_down_kernel,
        grid_spec=pltpu.PrefetchScalarGridSpec(
            num_scalar_prefetch=2,
            grid=(p // tm,),
            in_specs=[pl.BlockSpec((tm, f), rmap),
                      pl.BlockSpec((1, 1, f, d), wmap)],
            out_specs=pl.BlockSpec((tm, half), rmap),
            scratch_shapes=[pltpu.VMEM((f, d), BF16)]),
        out_shape=jax.ShapeDtypeStruct((p, half), jnp.uint32),
        compiler_params=_cparams("arbitrary"),
        name="moe_down",
    )(tile_e, tile_valid, act, w_d)


def _combine_kernel(n_lat, tm, tile0, z_ref, ya_ref, yb_ref, info_ref, gl_ref, gc_ref, o_ref):
    is_lat = (pl.program_id(0) + tile0) * tm < n_lat
    gate = jnp.where(is_lat, gl_ref[0], gc_ref[0])
    w1 = info_ref[:, R_WT:R_WT + 1]
    w2 = info_ref[:, R_WT + 1:R_WT + 2]
    half = ya_ref.shape[1]
    a_lo, a_hi = _unpack_pairs(ya_ref[...])
    b_lo, b_hi = _unpack_pairs(yb_ref[...])
    o_ref[:, :half] = z_ref[:, :half] + gate[:, :half] * (w1 * a_lo + w2 * b_lo)
    o_ref[:, half:] = z_ref[:, half:] + gate[:, half:] * (w1 * a_hi + w2 * b_hi)


def _combine(z, y2, info, mods, gate_idx, row0, part_rows, n_lat):
    d = z.shape[1]
    tm = MOE_TILE
    assert row0 % tm == 0 and part_rows % tm == 0 and n_lat % tm == 0
    nt = part_rows // tm
    t0 = row0 // tm
    return pl.pallas_call(
        functools.partial(_combine_kernel, n_lat, tm, t0),
        grid=(nt,),
        in_specs=[pl.BlockSpec((tm, d), lambda i: (i + t0, 0)),
                  pl.BlockSpec((tm, d // 2), lambda i: (i, 0)),
                  pl.BlockSpec((tm, d // 2), lambda i: (i + nt, 0)),
                  pl.BlockSpec((tm, LANES), lambda i: (i + t0, 0)),
                  pl.BlockSpec((1, 1, d), lambda i: (gate_idx, 0, 0)),
                  pl.BlockSpec((1, 1, d), lambda i: (N_MOD + gate_idx, 0, 0))],
        out_specs=pl.BlockSpec((tm, d), lambda i: (i + t0, 0)),
        out_shape=jax.ShapeDtypeStruct(z.shape, F32),
        input_output_aliases={0: 0},
        compiler_params=_cparams("parallel"),
        name="moe_combine",
    )(z, y2, y2, info, mods, mods)


def _rows_of(x, idx):
    return x.at[idx].get(mode="promise_in_bounds", unique_indices=False)


def _moe(h2, h2_pairs, z, mods, w_r, w_g, w_u, w_d, li, n_groups, per_group, rows, n_lat):
    n_exp = n_groups * per_group
    tm = MOE_TILE
    first = (rows // tm // MOE_PARTS) * tm
    part_starts = tuple(range(0, first * MOE_PARTS, first))
    info, cnt = _router(h2, w_r, n_groups, per_group, rows, part_starts)
    experts = jnp.arange(n_exp, dtype=jnp.int32)
    for part, row0 in enumerate(part_starts):
        part_rows = (part_starts[part + 1] if part + 1 < len(part_starts) else rows) - row0
        eid = info[row0:row0 + part_rows, R_EID:R_EID + EXPERT_TOPK].astype(jnp.int32)
        rank = info[row0:row0 + part_rows, R_RANK:R_RANK + EXPERT_TOPK].astype(jnp.int32)
        counts = cnt[8 * part, n_groups:n_groups + n_exp].astype(jnp.int32)
        padded = ((counts + tm - 1) // tm) * tm
        ends = jnp.sum(jnp.where(experts[None, :] <= experts[:, None], padded[None, :], 0), axis=1)
        starts = ends - padded
        pos = jnp.sum(jnp.where(eid[:, :, None] == experts[None, None, :], starts[None, None, :], 0),
                      axis=2) + rank
        pos = pos.T.reshape(-1)
        n_pairs = part_rows * EXPERT_TOPK
        n_tiles = (n_pairs + n_exp * (tm - 1)) // tm + 1
        p = n_tiles * tm
        pair_token = row0 + jnp.arange(n_pairs, dtype=jnp.int32) % part_rows
        filler = row0 + jnp.arange(p, dtype=jnp.int32) % part_rows
        row_token = filler.at[pos].set(pair_token, mode="promise_in_bounds", unique_indices=True)
        tile_start = jnp.arange(n_tiles, dtype=jnp.int32) * tm
        tile_e = jnp.minimum(jnp.sum((ends[None, :] <= tile_start[:, None]).astype(jnp.int32), axis=1),
                             n_exp - 1)
        tile_valid = (tile_start < ends[-1]).astype(jnp.int32)
        xs = _rows_of(h2_pairs, row_token)
        ys = _expert_ffn(xs, tile_e, tile_valid, w_g, w_u, w_d, li, tm)
        z = _combine(z, _rows_of(ys, pos), info, mods, 5, row0, part_rows, n_lat)
    return z


def _rope_tables(n_lat, n_ctx):
    t = jnp.arange(n_lat)
    row = (t // GRID_W).astype(F32)
    col = (t % GRID_W).astype(F32)
    inv = ROPE_THETA ** (-jnp.arange(ROPE_PAIRS, dtype=F32) / ROPE_PAIRS)
    ar = row[:, None] * inv
    ac = col[:, None] * inv
    cos = jnp.concatenate([jnp.cos(ar), jnp.cos(ar), jnp.cos(ac), jnp.cos(ac)], axis=1)
    sin = jnp.concatenate([-jnp.sin(ar), jnp.sin(ar), -jnp.sin(ac), jnp.sin(ac)], axis=1)
    cos = jnp.concatenate([cos, jnp.ones((n_ctx, HEAD_DIM), F32)], axis=0)
    sin = jnp.concatenate([sin, jnp.zeros((n_ctx, HEAD_DIM), F32)], axis=0)
    return cos, sin


def _with_ctx(o_lat, o_ctx, n_lat):
    return lax.dynamic_update_slice(o_lat, o_ctx, (n_lat, 0))


def kernel(x, c, ctx, c_ctx, ada_down, ada_up, ada_bias, norm1_g, norm2_g, w_in_ab, w_out_ab, qnorm_a,
           knorm_a, sink_b, w_in_cd, w_out_cd, rpb_c, dw_w, dw_b, ln_d_g, ln_d_b, router_group,
           router_expert, w_gate, w_up, w_down, final_g):
    assert x.shape[0] == 1 and ctx.shape[0] == 1
    n_lat, d = x.shape[1], x.shape[2]
    n_ctx = ctx.shape[1]
    n_all = n_lat + n_ctx
    depth = ada_down.shape[0]
    assert n_lat % n_ctx == 0 and n_ctx % 128 == 0
    n_groups, per_group = router_expert.shape[1], router_expert.shape[3]
    n_mix = d // HEAD_DIM
    a_w = (n_mix // 2) * HEAD_DIM
    akv_w = (n_mix // 8) * HEAD_DIM
    c_w = (n_mix // 2) * HEAD_DIM
    d_ch = d // 2
    qscale = HEAD_DIM ** -0.5 * LOG2E

    z = jnp.concatenate([x[0], ctx[0]], axis=0)
    cond = jnp.zeros((8, d), F32).at[0].set(c[0]).at[1].set(c_ctx)
    mods_all = _modulation_all(cond, ada_down, ada_up, ada_bias)
    tabs = _rope_tables(n_lat, n_ctx)

    for layer in range(depth):
        need_ctx = layer < depth - 1
        rows = n_all if need_ctx else n_lat
        mods = mods_all[layer, :2].reshape(2 * N_MOD, 1, d)
        h = _norm_mod(z, norm1_g[layer], mods, 0, 1, n_all, n_lat, BF16)
        i = layer // 2
        if layer % 2 == 0:
            w = w_in_ab
            sink = sink_b[i] * LOG2E
            col = 0
            qa = _proj_in(h, w, i, col, a_w, n_all, gain=qnorm_a[i], rope_tabs=tabs, scale=qscale)
            col += a_w
            ka = _proj_in(h, w, i, col, akv_w, n_all, gain=knorm_a[i], rope_tabs=tabs)
            col += akv_w
            va = _proj_in(h, w, i, col, akv_w, n_all, ones_ext=True)
            col += akv_w
            qb = _proj_in(h, w, i, col, a_w, n_all, rope_tabs=tabs, scale=qscale)
            col += a_w
            kb = _proj_in(h, w, i, col, akv_w, n_all, rope_tabs=tabs)
            col += akv_w
            vb = _proj_in(h, w, i, col, akv_w, n_all, ones_ext=True)
            o1 = _global_attn(qa, ka, va, n_lat, n_ctx, rows)
            o2 = _window_attn(qb, kb, vb, sink, n_lat, n_ctx, rows)
            if need_ctx:
                o1 = _with_ctx(o1, _ctx_attn(qa, ka, va, None, n_lat, n_ctx), n_lat)
                o2 = _with_ctx(o2, _ctx_attn(qb, kb, vb, sink, n_lat, n_ctx), n_lat)
            w_out = w_out_ab
        else:
            w = w_in_cd
            qc = _proj_in(h, w, i, 0, c_w, n_all, scale=qscale)
            kc = _proj_in(h, w, i, c_w, c_w, n_all)
            vc = _proj_in(h, w, i, 2 * c_w, c_w, n_all, ones_ext=True)
            u = _proj_glu(h, w, i, 3 * c_w, d_ch, n_all)
            bias = _na_bias_table(rpb_c[i], n_lat // GRID_W)
            o1 = _na_attn(qc, kc, vc, bias, n_lat, n_ctx, rows)
            if need_ctx:
                o1 = _with_ctx(o1, _ctx_attn(qc, kc, vc, None, n_lat, n_ctx), n_lat)
            o2 = _conformer_conv(u, dw_w[i], dw_b[i], ln_d_g[i], ln_d_b[i], rows, n_lat)
            w_out = w_out_cd
        z = _proj_out(o1, o2, w_out, i, z, mods, 2, rows, n_lat)
        h2, h2_pairs = _norm_mod(z, norm2_g[layer], mods, 3, 4, rows, n_lat, BF16, packed=True)
        w_r = jnp.concatenate(
            [router_group[layer], router_expert[layer].transpose(1, 0, 2).reshape(d, n_groups * per_group)],
            axis=1)
        w_r = jnp.pad(w_r, ((0, 0), (0, LANES - w_r.shape[1]))).astype(BF16)
        z = _moe(h2, h2_pairs, z, mods, w_r, w_gate, w_up, w_down, layer, n_groups, per_group, rows, n_lat)
    out = _norm_mod(z, final_g, None, 0, 0, n_lat, n_lat, F32)
    return out[None]
```
